```python
import jax, jax.numpy as jnp
from jax import lax
import numpy as np

D_MODEL = 2048
BATCH = 8
SEQ = 2048
DEPTH = 2

MEM_LEN = 256
RMS_EPS = 1e-6
CHUNK = 64
HGRN_WIDTH = D_MODEL // 2
HGRN_HEAD_K = 128
HGRN_HEADS = HGRN_WIDTH // HGRN_HEAD_K
HGRN_HEAD_V = HGRN_WIDTH // HGRN_HEADS
GLA_WIDTH = D_MODEL - HGRN_WIDTH
GLA_HEADS = 4
GLA_HEAD_V = GLA_WIDTH // GLA_HEADS
GLA_HEAD_K = GLA_HEAD_V // 2
GLA_GATE_RANK = 16
GLA_GATE_NORMALIZER = 16.0
AB_SPLITS = (
    HGRN_HEADS * HGRN_HEAD_K,
    HGRN_HEADS * HGRN_HEAD_K,
    HGRN_HEADS * HGRN_HEAD_V,
    HGRN_HEADS * HGRN_HEAD_V,
    GLA_HEADS * GLA_HEAD_K,
    GLA_HEADS * GLA_HEAD_K,
    GLA_HEADS * GLA_HEAD_V,
    GLA_HEADS * GLA_HEAD_V,
    GLA_GATE_RANK,
)
AB_IN = sum(AB_SPLITS)
SB_HEADS = 16
SB_HEAD_DIM = D_MODEL // SB_HEADS
SB_QBLOCK = 128
XA_HEADS = 4
XA_HEAD_DIM = D_MODEL // XA_HEADS
D_FF = 5632
CONV_W = 3
N_AB = (DEPTH + 1) // 2
N_C = DEPTH // 2

kernel_name = "hybrid_hgrn2_gla_stickbreaking_trunk"


def rms_norm(x, gain):
    x32 = x.astype(jnp.float32)
    y = x32 * lax.rsqrt(jnp.mean(x32 * x32, axis=-1, keepdims=True) + RMS_EPS)
    return (y * gain.astype(jnp.float32)).astype(x.dtype)


def chunked_gated_linear_attention(q, k, v, log_g):
    out_dtype = v.dtype
    b_, s_, h_, k_ = q.shape
    v_ = v.shape[-1]
    n_ = s_ // CHUNK

    def to_chunks(a):
        return a.astype(jnp.float32).reshape(b_, n_, CHUNK, h_, a.shape[-1]).transpose(0, 3, 1, 2, 4)

    qc, kc, vc, gc = to_chunks(q), to_chunks(k), to_chunks(v), to_chunks(log_g)
    b = jnp.cumsum(gc, axis=3)
    b_mid = b[:, :, :, CHUNK // 2 - 1:CHUNK // 2, :]
    b_last = b[:, :, :, CHUNK - 1:, :]
    causal = jnp.tril(jnp.ones((CHUNK, CHUNK), dtype=bool))
    scores = jnp.einsum('bhnck,bhndk->bhncd', qc * jnp.exp(b - b_mid), kc * jnp.exp(b_mid - b))
    scores = jnp.where(causal, scores, 0.0)
    o_intra = jnp.einsum('bhncd,bhndv->bhncv', scores, vc)
    q_dec = qc * jnp.exp(b)
    k_dec = kc * jnp.exp(b_last - b)
    decay = jnp.exp(b_last[:, :, :, 0, :])

    def step(state, inp):
        qd, kd, vv, dec = inp
        o = jnp.einsum('bhck,bhkv->bhcv', qd, state)
        state = state * dec[..., None] + jnp.einsum('bhck,bhcv->bhkv', kd, vv)
        return state, o

    xs = (jnp.moveaxis(q_dec, 2, 0), jnp.moveaxis(k_dec, 2, 0),
          jnp.moveaxis(vc, 2, 0), jnp.moveaxis(decay, 2, 0))
    state0 = jnp.zeros((b_, h_, k_, v_), jnp.float32)
    _, o_inter = lax.scan(step, state0, xs)
    o = o_intra + jnp.moveaxis(o_inter, 0, 2)
    return o.transpose(0, 2, 3, 1, 4).reshape(b_, s_, h_, v_).astype(out_dtype)


def hgrn2_gla_mixer(h, w_in, lb_logits, a_idx, hgrn_gain, w_gk, b_gk, gla_gain, w_out):
    b_, s_, _ = h.shape
    p = h @ w_in
    cuts = list(np.cumsum(AB_SPLITS)[:-1])
    a_q, a_f, a_i, a_g, g_q, g_k, g_v, g_g, g_low = jnp.split(p, cuts, axis=-1)

    lb = jnp.cumsum(jax.nn.softmax(lb_logits.astype(jnp.float32), axis=0), axis=0)[a_idx]
    f = lb + (1.0 - lb) * jax.nn.sigmoid(a_f.astype(jnp.float32))
    shp_k = (b_, s_, HGRN_HEADS, HGRN_HEAD_K)
    shp_v = (b_, s_, HGRN_HEADS, HGRN_HEAD_V)
    o_a = chunked_gated_linear_attention(
        jax.nn.silu(a_q).reshape(shp_k),
        (1.0 - f).reshape(shp_k),
        a_i.reshape(shp_v),
        jnp.log(f).reshape(shp_k))
    o_a = rms_norm(o_a, hgrn_gain) * jax.nn.sigmoid(a_g).reshape(shp_v)

    gk = jax.nn.log_sigmoid(g_low.astype(jnp.float32) @ w_gk.astype(jnp.float32)
                            + b_gk.astype(jnp.float32)) / GLA_GATE_NORMALIZER
    shp_gk = (b_, s_, GLA_HEADS, GLA_HEAD_K)
    shp_gv = (b_, s_, GLA_HEADS, GLA_HEAD_V)
    o_b = chunked_gated_linear_attention(
        g_q.reshape(shp_gk) * (GLA_HEAD_K ** -0.5),
        g_k.reshape(shp_gk),
        g_v.reshape(shp_gv),
        gk.reshape(shp_gk))
    o_b = rms_norm(o_b, gla_gain) * jax.nn.silu(g_g).reshape(shp_gv)

    o = jnp.concatenate([o_a.reshape(b_, s_, HGRN_WIDTH), o_b.reshape(b_, s_, GLA_WIDTH)], axis=-1)
    return o @ w_out


def stick_breaking_mixer(h, w_qkv, w_out):
    b_, s_, _ = h.shape
    qkv = (h @ w_qkv).reshape(b_, s_, 3, SB_HEADS, SB_HEAD_DIM).transpose(2, 0, 3, 1, 4)
    q, k, v = qkv[0], qkv[1], qkv[2]
    scale = SB_HEAD_DIM ** -0.5
    outs = []
    for blk in range(s_ // SB_QBLOCK):
        t0 = blk * SB_QBLOCK
        t1 = t0 + SB_QBLOCK
        z = jnp.einsum('bhtd,bhsd->bhts', q[:, :, t0:t1], k[:, :, :t1]).astype(jnp.float32) * scale
        t_idx = t0 + jnp.arange(SB_QBLOCK)
        s_idx = jnp.arange(t1)
        mask = s_idx[None, :] < t_idx[:, None]
        sp = jnp.where(mask, jax.nn.softplus(z), 0.0)
        log_a = z - lax.cumsum(sp, axis=3, reverse=True)
        att = jnp.exp(jnp.where(mask, log_a, -jnp.inf))
        outs.append(jnp.einsum('bhts,bhsd->bhtd', att.astype(v.dtype), v[:, :, :t1]))
    o = jnp.concatenate(outs, axis=2)
    return o.transpose(0, 2, 1, 3).reshape(b_, s_, D_MODEL) @ w_out


def memory_cross_attention(h, mem_n, w_q, w_kv, w_o):
    b_, s_, _ = h.shape
    m_ = mem_n.shape[1]
    q = (h @ w_q).reshape(b_, s_, XA_HEADS, XA_HEAD_DIM)
    kv = (mem_n @ w_kv).reshape(b_, m_, 2, XA_HEADS, XA_HEAD_DIM)
    k, v = kv[:, :, 0], kv[:, :, 1]
    scores = jnp.einsum('bshd,bmhd->bhsm', q, k).astype(jnp.float32) * (XA_HEAD_DIM ** -0.5)
    probs = jax.nn.softmax(scores, axis=-1).astype(v.dtype)
    o = jnp.einsum('bhsm,bmhd->bshd', probs, v).reshape(b_, s_, D_MODEL)
    return o @ w_o


def conv_ffn(h, w_in, conv_w, conv_b, w_out):
    s_ = h.shape[1]
    u, g = jnp.split(h @ w_in, 2, axis=-1)
    gp = jnp.pad(g, ((0, 0), (CONV_W - 1, 0), (0, 0)))
    gc = conv_b + sum(conv_w[j] * gp[:, j:j + s_] for j in range(CONV_W))
    return (jax.nn.silu(gc) * u) @ w_out


def setup_inputs(seed: int = 0) -> dict:
    key = jax.random.key(seed)
    ks = jax.random.split(key, 24)

    def nrm(k, shape, fan_in):
        return jax.random.normal(k, shape, jnp.float32) * (fan_in ** -0.5)

    def gain(k, shape):
        return 1.0 + 0.02 * jax.random.normal(k, shape, jnp.float32)

    return {
        "x": jax.random.normal(ks[0], (BATCH, SEQ, D_MODEL), jnp.float32),
        "mem": jax.random.normal(ks[1], (BATCH, MEM_LEN, D_MODEL), jnp.float32),
        "mem_norm": gain(ks[2], (D_MODEL,)),
        "norm_mix": gain(ks[3], (DEPTH, D_MODEL)),
        "norm_xattn": gain(ks[4], (DEPTH, D_MODEL)),
        "norm_ffn": gain(ks[5], (DEPTH, D_MODEL)),
        "ab_w_in": nrm(ks[6], (N_AB, D_MODEL, AB_IN), D_MODEL),
        "hgrn_lb_logits": jax.random.normal(ks[7], (N_AB + 1, HGRN_HEADS * HGRN_HEAD_K), jnp.float32),
        "hgrn_norm": gain(ks[8], (N_AB, HGRN_HEAD_V)),
        "gla_w_gk": nrm(ks[9], (N_AB, GLA_GATE_RANK, GLA_HEADS * GLA_HEAD_K), GLA_GATE_RANK),
        "gla_b_gk": 0.01 * jax.random.normal(ks[10], (N_AB, GLA_HEADS * GLA_HEAD_K), jnp.float32),
        "gla_norm": gain(ks[11], (N_AB, GLA_HEAD_V)),
        "ab_w_out": nrm(ks[12], (N_AB, D_MODEL, D_MODEL), D_MODEL),
        "sb_w_qkv": nrm(ks[13], (N_C, D_MODEL, 3 * D_MODEL), D_MODEL),
        "sb_w_out": nrm(ks[14], (N_C, D_MODEL, D_MODEL), D_MODEL),
        "xa_w_q": nrm(ks[15], (DEPTH, D_MODEL, D_MODEL), D_MODEL),
        "xa_w_kv": nrm(ks[16], (DEPTH, D_MODEL, 2 * D_MODEL), D_MODEL),
        "xa_w_o": nrm(ks[17], (DEPTH, D_MODEL, D_MODEL), D_MODEL),
        "ffn_w_in": nrm(ks[18], (DEPTH, D_MODEL, 2 * D_FF), D_MODEL),
        "ffn_conv_w": nrm(ks[19], (DEPTH, CONV_W, D_FF), CONV_W),
        "ffn_conv_b": 0.01 * jax.random.normal(ks[20], (DEPTH, D_FF), jnp.float32),
        "ffn_w_out": nrm(ks[21], (DEPTH, D_FF, D_MODEL), D_FF),
        "final_norm": gain(ks[22], (D_MODEL,)),
    }


def reference(x, mem, mem_norm, norm_mix, norm_xattn, norm_ffn, ab_w_in, hgrn_lb_logits,
              hgrn_norm, gla_w_gk, gla_b_gk, gla_norm, ab_w_out, sb_w_qkv, sb_w_out,
              xa_w_q, xa_w_kv, xa_w_o, ffn_w_in, ffn_conv_w, ffn_conv_b, ffn_w_out, final_norm):
    mem_n = rms_norm(mem, mem_norm)
    for layer in range(DEPTH):
        h = rms_norm(x, norm_mix[layer])
        if layer % 2 == 0:
            a = layer // 2
            x = x + hgrn2_gla_mixer(h, ab_w_in[a], hgrn_lb_logits, a, hgrn_norm[a],
                                    gla_w_gk[a], gla_b_gk[a], gla_norm[a], ab_w_out[a])
        else:
            c = layer // 2
            x = x + stick_breaking_mixer(h, sb_w_qkv[c], sb_w_out[c])
        x = x + memory_cross_attention(rms_norm(x, norm_xattn[layer]), mem_n,
                                       xa_w_q[layer], xa_w_kv[layer], xa_w_o[layer])
        x = x + conv_ffn(rms_norm(x, norm_ffn[layer]), ffn_w_in[layer], ffn_conv_w[layer],
                         ffn_conv_b[layer], ffn_w_out[layer])
    return rms_norm(x, final_norm)
```

```python
import functools

import jax
import jax.numpy as jnp
from jax import lax
from jax.experimental import pallas as pl
from jax.experimental.pallas import tpu as pltpu

F32 = jnp.float32
BF16 = jnp.bfloat16

RMS_EPS = 1e-6
CHUNK = 64
HGRN_HEADS = 8
HGRN_HEAD_DIM = 128
GLA_HEADS = 4
GLA_HEAD_K = 128
GLA_HEAD_V = 256
GLA_GATE_RANK = 16
GLA_GATE_NORMALIZER = 16.0
SB_HEADS = 16
SB_HEAD_DIM = 128
XA_HEADS = 4
CONV_W = 3

LANES = 128
BF16_SUBLANES = 16
VMEM_LIMIT_BYTES = 52 * 1024 * 1024


def _params(*semantics):
    return pltpu.CompilerParams(dimension_semantics=semantics, vmem_limit_bytes=VMEM_LIMIT_BYTES)


def _rms(x, gain):
    return x * lax.rsqrt(jnp.mean(x * x, axis=-1, keepdims=True) + RMS_EPS) * gain


def _sigmoid(x):
    return 1.0 / (1.0 + jnp.exp(-x))


def _softplus(x):
    return jnp.maximum(x, 0.0) + jnp.log(1.0 + jnp.exp(-jnp.abs(x)))


def _dot(a, b):
    return jnp.dot(a, b, preferred_element_type=F32)


def _dot_nt(a, b):
    return lax.dot_general(a, b, (((1,), (1,)), ((), ())), preferred_element_type=F32)


def _dot_tn(a, b):
    return lax.dot_general(a, b, (((0,), (0,)), ((), ())), preferred_element_type=F32)


def _split3(x):
    hi = x.astype(BF16)
    r1 = x - hi.astype(F32)
    mid = r1.astype(BF16)
    lo = (r1 - mid.astype(F32)).astype(BF16)
    return hi, mid, lo


def _norm_matmul_kernel(x_ref, g_ref, w_ref, o_ref, h_ref):
    @pl.when(pl.program_id(1) == 0)
    def _():
        h_ref[...] = _rms(x_ref[...], g_ref[...]).astype(BF16)

    o_ref[...] = _dot(h_ref[...], w_ref[...]).astype(o_ref.dtype)


def norm_matmul(x, gain, w, out_dtype, tm, tn):
    m, k = x.shape
    n = w.shape[1]
    return pl.pallas_call(
        _norm_matmul_kernel,
        grid=(m // tm, n // tn),
        in_specs=[
            pl.BlockSpec((tm, k), lambda i, j: (i, 0)),
            pl.BlockSpec((1, k), lambda i, j: (0, 0)),
            pl.BlockSpec((k, tn), lambda i, j: (0, j)),
        ],
        out_specs=pl.BlockSpec((tm, tn), lambda i, j: (i, j)),
        out_shape=jax.ShapeDtypeStruct((m, n), out_dtype),
        scratch_shapes=[pltpu.VMEM((tm, k), BF16)],
        compiler_params=_params("parallel", "arbitrary"),
        name="norm_matmul",
    )(x, gain, w)


def _ab_proj_kernel(x_ref, g_ref, w_ref, wlow_ref, wgk_ref, bgk_ref, o_ref, gk_ref, h_ref):
    @pl.when(pl.program_id(1) == 0)
    def _():
        h = _rms(x_ref[...], g_ref[...]).astype(BF16)
        h_ref[...] = h
        low = _dot(h, wlow_ref[...])
        pre = _dot(low.astype(BF16), wgk_ref[...]) + bgk_ref[...]
        gk_ref[...] = -_softplus(-pre) * (1.0 / GLA_GATE_NORMALIZER)

    o_ref[...] = _dot(h_ref[...], w_ref[...])


def ab_proj(x, gain, w, w_low, w_gk, b_gk, tm, tn):
    m, k = x.shape
    n = w.shape[1]
    ngk = w_gk.shape[1]
    return pl.pallas_call(
        _ab_proj_kernel,
        grid=(m // tm, n // tn),
        in_specs=[
            pl.BlockSpec((tm, k), lambda i, j: (i, 0)),
            pl.BlockSpec((1, k), lambda i, j: (0, 0)),
            pl.BlockSpec((k, tn), lambda i, j: (0, j)),
            pl.BlockSpec((k, LANES), lambda i, j: (0, 0)),
            pl.BlockSpec((LANES, ngk), lambda i, j: (0, 0)),
            pl.BlockSpec((1, ngk), lambda i, j: (0, 0)),
        ],
        out_specs=[
            pl.BlockSpec((tm, tn), lambda i, j: (i, j)),
            pl.BlockSpec((tm, ngk), lambda i, j: (i, 0)),
        ],
        out_shape=[
            jax.ShapeDtypeStruct((m, n), F32),
            jax.ShapeDtypeStruct((m, ngk), F32),
        ],
        scratch_shapes=[pltpu.VMEM((tm, k), BF16)],
        compiler_params=_params("parallel", "arbitrary"),
        name="ab_proj",
    )(x, gain, w, w_low, w_gk, b_gk)


def _matmul_resid_kernel(a_ref, w_ref, r_ref, o_ref):
    o_ref[...] = r_ref[...] + _dot(a_ref[...], w_ref[...])


def matmul_resid(a, w, resid, tm, tn):
    m, k = a.shape
    n = w.shape[1]
    return pl.pallas_call(
        _matmul_resid_kernel,
        grid=(m // tm, n // tn),
        in_specs=[
            pl.BlockSpec((tm, k), lambda i, j: (i, 0)),
            pl.BlockSpec((k, tn), lambda i, j: (0, j)),
            pl.BlockSpec((tm, tn), lambda i, j: (i, j)),
        ],
        out_specs=pl.BlockSpec((tm, tn), lambda i, j: (i, j)),
        out_shape=jax.ShapeDtypeStruct((m, n), F32),
        compiler_params=_params("parallel", "arbitrary"),
        name="matmul_resid",
    )(a, w, resid)


def _matmul2_resid_kernel(a1_ref, a2_ref, w1_ref, w2_ref, r_ref, o_ref):
    o_ref[...] = r_ref[...] + (_dot(a1_ref[...], w1_ref[...]) + _dot(a2_ref[...], w2_ref[...]))


def matmul2_resid(a1, a2, w, resid, tm, tn):
    m, kh = a1.shape
    n = w.shape[1]
    return pl.pallas_call(
        _matmul2_resid_kernel,
        grid=(m // tm, n // tn),
        in_specs=[
            pl.BlockSpec((tm, kh), lambda i, j: (i, 0)),
            pl.BlockSpec((tm, kh), lambda i, j: (i, 0)),
            pl.BlockSpec((kh, tn), lambda i, j: (0, j)),
            pl.BlockSpec((kh, tn), lambda i, j: (1, j)),
            pl.BlockSpec((tm, tn), lambda i, j: (i, j)),
        ],
        out_specs=pl.BlockSpec((tm, tn), lambda i, j: (i, j)),
        out_shape=jax.ShapeDtypeStruct((m, n), F32),
        compiler_params=_params("parallel", "arbitrary"),
        name="matmul2_resid",
    )(a1, a2, w, w, resid)


def _linear_attention_block(q, k, v, g, state_ref):
    t = q.shape[0]
    row = lax.broadcasted_iota(jnp.int32, (CHUNK, CHUNK), 0)
    col = lax.broadcasted_iota(jnp.int32, (CHUNK, CHUNK), 1)
    causal = row >= col
    tri = causal.astype(BF16)
    outs = []
    state = state_ref[...]
    for c in range(t // CHUNK):
        sl = slice(c * CHUNK, (c + 1) * CHUNK)
        qc, kc, vc, gc = q[sl], k[sl], v[sl], g[sl]
        g_hi, g_mid, g_lo = _split3(gc)
        b = _dot(tri, g_hi) + _dot(tri, g_mid) + _dot(tri, g_lo)
        b_mid = b[CHUNK // 2 - 1:CHUNK // 2]
        b_last = b[CHUNK - 1:]
        vb = vc.astype(BF16)
        qs = (qc * jnp.exp(b - b_mid)).astype(BF16)
        ks = (kc * jnp.exp(b_mid - b)).astype(BF16)
        scores = jnp.where(causal, _dot_nt(qs, ks), 0.0)
        o_intra = _dot(scores.astype(BF16), vb)
        qd = (qc * jnp.exp(b)).astype(BF16)
        kd = (kc * jnp.exp(b_last - b)).astype(BF16)
        o_inter = _dot_nt(qd, state.astype(BF16))
        outs.append(o_intra + o_inter)
        state = state * jnp.exp(b_last) + _dot_tn(vb, kd)
    state_ref[...] = state
    return jnp.concatenate(outs, axis=0)


def _hgrn_kernel(aq_ref, af_ref, ai_ref, ag_ref, lbl_ref, gain_ref, o_ref, state_ref, *, lb_rows):
    @pl.when(pl.program_id(2) == 0)
    def _():
        state_ref[...] = jnp.zeros_like(state_ref)

    logits = lbl_ref[...]
    e = jnp.exp(logits - jnp.max(logits, axis=0, keepdims=True))
    lb = jnp.sum(e[:lb_rows], axis=0, keepdims=True) / jnp.sum(e, axis=0, keepdims=True)

    a_q = aq_ref[0]
    f = lb + (1.0 - lb) * _sigmoid(af_ref[0])
    o = _linear_attention_block(a_q * _sigmoid(a_q), 1.0 - f, ai_ref[0], jnp.log(f), state_ref)
    o_ref[0] = (_rms(o, gain_ref[...]) * _sigmoid(ag_ref[0])).astype(o_ref.dtype)


def hgrn_mixer(p, lb_logits, a_idx, gain, ts):
    b, s, _ = p.shape
    nh, hd = HGRN_HEADS, HGRN_HEAD_DIM
    nlb = lb_logits.shape[0]

    def col(part):
        return pl.BlockSpec((1, ts, hd), lambda bi, h, si: (bi, si, part * nh + h))

    return pl.pallas_call(
        functools.partial(_hgrn_kernel, lb_rows=a_idx + 1),
        grid=(b, nh, s // ts),
        in_specs=[
            col(0), col(1), col(2), col(3),
            pl.BlockSpec((nlb, hd), lambda bi, h, si: (0, h)),
            pl.BlockSpec((1, hd), lambda bi, h, si: (0, 0)),
        ],
        out_specs=pl.BlockSpec((1, ts, hd), lambda bi, h, si: (bi, si, h)),
        out_shape=jax.ShapeDtypeStruct((b, s, nh * hd), BF16),
        scratch_shapes=[pltpu.VMEM((hd, hd), F32)],
        compiler_params=_params("parallel", "parallel", "arbitrary"),
        name="hgrn_mixer",
    )(p, p, p, p, lb_logits, gain)


def _gla_kernel(q_ref, k_ref, v_ref, gg_ref, gk_ref, gain_ref, o_ref, state_ref):
    @pl.when(pl.program_id(2) == 0)
    def _():
        state_ref[...] = jnp.zeros_like(state_ref)

    q = q_ref[0] * (GLA_HEAD_K ** -0.5)
    o = _linear_attention_block(q, k_ref[0], v_ref[0], gk_ref[0], state_ref)
    g_g = gg_ref[0]
    o_ref[0] = (_rms(o, gain_ref[...]) * (g_g * _sigmoid(g_g))).astype(o_ref.dtype)


def gla_mixer(p, gk, gain, ts):
    b, s, _ = p.shape
    nh, kd, vd = GLA_HEADS, GLA_HEAD_K, GLA_HEAD_V
    base = 4 * HGRN_HEADS * HGRN_HEAD_DIM
    q0 = base // kd
    k0 = q0 + nh
    v0 = (base + 2 * nh * kd) // vd
    g0 = v0 + nh
    return pl.pallas_call(
        _gla_kernel,
        grid=(b, nh, s // ts),
        in_specs=[
            pl.BlockSpec((1, ts, kd), lambda bi, h, si: (bi, si, q0 + h)),
            pl.BlockSpec((1, ts, kd), lambda bi, h, si: (bi, si, k0 + h)),
            pl.BlockSpec((1, ts, vd), lambda bi, h, si: (bi, si, v0 + h)),
            pl.BlockSpec((1, ts, vd), lambda bi, h, si: (bi, si, g0 + h)),
            pl.BlockSpec((1, ts, kd), lambda bi, h, si: (bi, si, h)),
            pl.BlockSpec((1, vd), lambda bi, h, si: (0, 0)),
        ],
        out_specs=pl.BlockSpec((1, ts, vd), lambda bi, h, si: (bi, si, h)),
        out_shape=jax.ShapeDtypeStruct((b, s, nh * vd), BF16),
        scratch_shapes=[pltpu.VMEM((vd, kd), F32)],
        compiler_params=_params("parallel", "parallel", "arbitrary"),
        name="gla_mixer",
    )(p, p, p, p, gk, gain)


def _sb_kernel(q_ref, k_ref, v_ref, o_ref, *, tq):
    qi = pl.program_id(2)
    q = q_ref[0]
    scale = SB_HEAD_DIM ** -0.5
    row = lax.broadcasted_iota(jnp.int32, (tq, tq), 0)
    col = lax.broadcasted_iota(jnp.int32, (tq, tq), 1)
    suffix_ones = (row >= col).astype(BF16)
    strictly_causal = col < row

    def key_block(j, acc, later, diagonal):
        start = pl.multiple_of(j * tq, tq)
        kb = k_ref[0, pl.ds(start, tq), :]
        vb = v_ref[0, pl.ds(start, tq), :]
        z = _dot_nt(q, kb) * scale
        sp = _softplus(z)
        if diagonal:
            sp = jnp.where(strictly_causal, sp, 0.0)
        sp_hi = sp.astype(BF16)
        sp_lo = (sp - sp_hi.astype(F32)).astype(BF16)
        cum = _dot(sp_hi, suffix_ones) + _dot(sp_lo, suffix_ones) + later
        log_a = z - cum
        if diagonal:
            log_a = jnp.where(strictly_causal, log_a, -jnp.inf)
        acc = acc + _dot(jnp.exp(log_a).astype(BF16), vb)
        return acc, cum[:, 0:1]

    acc, later = key_block(qi, jnp.zeros((tq, SB_HEAD_DIM), F32), jnp.zeros((tq, 1), F32), True)

    def body(t, carry):
        return key_block(qi - 1 - t, carry[0], carry[1], False)

    acc, later = lax.fori_loop(0, qi, body, (acc, later))
    o_ref[0] = acc.astype(o_ref.dtype)


def sb_attention(qkv, tq):
    b, s, _ = qkv.shape
    nh, hd = SB_HEADS, SB_HEAD_DIM
    return pl.pallas_call(
        functools.partial(_sb_kernel, tq=tq),
        grid=(b, nh, s // tq),
        in_specs=[
            pl.BlockSpec((1, tq, hd), lambda bi, h, qi: (bi, qi, h)),
            pl.BlockSpec((1, s, hd), lambda bi, h, qi: (bi, 0, nh + h)),
            pl.BlockSpec((1, s, hd), lambda bi, h, qi: (bi, 0, 2 * nh + h)),
        ],
        out_specs=pl.BlockSpec((1, tq, hd), lambda bi, h, qi: (bi, qi, h)),
        out_shape=jax.ShapeDtypeStruct((b, s, nh * hd), BF16),
        compiler_params=_params("parallel", "parallel", "arbitrary"),
        name="sb_attention",
    )(qkv, qkv, qkv)


def _xattn_kernel(q_ref, k_ref, v_ref, o_ref):
    q = q_ref[0]
    scores = _dot_nt(q, k_ref[0]) * (q.shape[-1] ** -0.5)
    e = jnp.exp(scores - jnp.max(scores, axis=-1, keepdims=True))
    probs = e / jnp.sum(e, axis=-1, keepdims=True)
    o_ref[0] = _dot(probs.astype(BF16), v_ref[0]).astype(o_ref.dtype)


def cross_attention(q, kv, layer, tq):
    b, s, d = q.shape
    m = kv.shape[1]
    nh = XA_HEADS
    hd = d // nh
    k0 = layer * 2 * nh
    return pl.pallas_call(
        _xattn_kernel,
        grid=(b, nh, s // tq),
        in_specs=[
            pl.BlockSpec((1, tq, hd), lambda bi, h, si: (bi, si, h)),
            pl.BlockSpec((1, m, hd), lambda bi, h, si: (bi, 0, k0 + h)),
            pl.BlockSpec((1, m, hd), lambda bi, h, si: (bi, 0, k0 + nh + h)),
        ],
        out_specs=pl.BlockSpec((1, tq, hd), lambda bi, h, si: (bi, si, h)),
        out_shape=jax.ShapeDtypeStruct((b, s, d), BF16),
        compiler_params=_params("parallel", "parallel", "arbitrary"),
        name="cross_attention",
    )(q, kv, kv)


HALO = BF16_SUBLANES


def _ffn_in_kernel(x_ref, halo_ref, g_ref, wu_ref, wg_ref, cw_ref, cb_ref, o_ref, h_ref, gs_ref,
                   *, tm, blocks_per_seq):
    inside = pl.program_id(0) % blocks_per_seq != 0

    @pl.when(pl.program_id(1) == 0)
    def _():
        h_ref[HALO:, :] = _rms(x_ref[...], g_ref[...]).astype(BF16)
        hh = _rms(halo_ref[...], g_ref[...])
        h_ref[:HALO, :] = jnp.where(inside, hh, 0.0).astype(BF16)

    u = _dot(h_ref[HALO:, :], wu_ref[...])
    gs_ref[...] = _dot(h_ref[...], wg_ref[...])
    cw = cw_ref[...]
    conv = 0.0
    for tap in range(CONV_W):
        lo = HALO - (CONV_W - 1) + tap
        conv = conv + cw[tap:tap + 1] * gs_ref[lo:lo + tm, :]
    gc = cb_ref[...] + conv
    o_ref[...] = (gc * _sigmoid(gc) * u).astype(o_ref.dtype)


def ffn_in(x, gain, w_in, conv_w, conv_b, seq, tm, tn):
    m, d = x.shape
    f = w_in.shape[1] // 2
    nj = f // tn
    halo_per_block = tm // HALO
    return pl.pallas_call(
        functools.partial(_ffn_in_kernel, tm=tm, blocks_per_seq=seq // tm),
        grid=(m // tm, nj),
        in_specs=[
            pl.BlockSpec((tm, d), lambda i, j: (i, 0)),
            pl.BlockSpec((HALO, d), lambda i, j: (jnp.maximum(i * halo_per_block - 1, 0), 0)),
            pl.BlockSpec((1, d), lambda i, j: (0, 0)),
            pl.BlockSpec((d, tn), lambda i, j: (0, j)),
            pl.BlockSpec((d, tn), lambda i, j: (0, nj + j)),
            pl.BlockSpec((CONV_W, tn), lambda i, j: (0, j)),
            pl.BlockSpec((1, tn), lambda i, j: (0, j)),
        ],
        out_specs=pl.BlockSpec((tm, tn), lambda i, j: (i, j)),
        out_shape=jax.ShapeDtypeStruct((m, f), BF16),
        scratch_shapes=[pltpu.VMEM((HALO + tm, d), BF16), pltpu.VMEM((HALO + tm, tn), F32)],
        compiler_params=_params("parallel", "arbitrary"),
        name="ffn_in",
    )(x, x, gain, w_in, w_in, conv_w, conv_b)


def _norm_kernel(x_ref, g_ref, o_ref):
    o_ref[...] = _rms(x_ref[...], g_ref[...])


def rms_norm_rows(x, gain, tm):
    m, d = x.shape
    return pl.pallas_call(
        _norm_kernel,
        grid=(m // tm,),
        in_specs=[pl.BlockSpec((tm, d), lambda i: (i, 0)), pl.BlockSpec((1, d), lambda i: (0, 0))],
        out_specs=pl.BlockSpec((tm, d), lambda i: (i, 0)),
        out_shape=jax.ShapeDtypeStruct((m, d), F32),
        compiler_params=_params("parallel"),
        name="final_norm",
    )(x, gain)


def kernel(x, mem, mem_norm, norm_mix, norm_xattn, norm_ffn, ab_w_in, hgrn_lb_logits, hgrn_norm, gla_w_gk, gla_b_gk, gla_norm, ab_w_out, sb_w_qkv, sb_w_out, xa_w_q, xa_w_kv, xa_w_o, ffn_w_in, ffn_conv_w, ffn_conv_b, ffn_w_out, final_norm):
    b, s, d = x.shape
    depth = norm_mix.shape[0]
    m_len = mem.shape[1]
    rows = b * s
    tm = min(512, s)
    ts = min(512, s)
    tq_sb = min(256, s)

    def row(v):
        return v.reshape(1, -1)

    xr = x.reshape(rows, d)

    w_kv_all = jnp.concatenate([xa_w_kv[l] for l in range(depth)], axis=1).astype(BF16)
    kv = norm_matmul(mem.reshape(b * m_len, d), row(mem_norm), w_kv_all, BF16,
                     tm=min(512, b * m_len), tn=1024)
    kv = kv.reshape(b, m_len, -1)

    ab_main = 4 * HGRN_HEADS * HGRN_HEAD_DIM + GLA_HEADS * (2 * GLA_HEAD_K + 2 * GLA_HEAD_V)

    for layer in range(depth):
        gain = row(norm_mix[layer])
        if layer % 2 == 0:
            a = layer // 2
            w_in = ab_w_in[a]
            w_low = jnp.pad(w_in[:, ab_main:], ((0, 0), (0, LANES - GLA_GATE_RANK))).astype(BF16)
            w_gk = jnp.pad(gla_w_gk[a], ((0, LANES - GLA_GATE_RANK), (0, 0))).astype(BF16)
            p, gk = ab_proj(xr, gain, w_in[:, :ab_main].astype(BF16), w_low, w_gk,
                            row(gla_b_gk[a]), tm=tm, tn=1024)
            p = p.reshape(b, s, ab_main)
            o_a = hgrn_mixer(p, hgrn_lb_logits, a, row(hgrn_norm[a]), ts)
            o_b = gla_mixer(p, gk.reshape(b, s, -1), row(gla_norm[a]), ts)
            xr = matmul2_resid(o_a.reshape(rows, -1), o_b.reshape(rows, -1),
                               ab_w_out[a].astype(BF16), xr, tm=tm, tn=d)
        else:
            c = layer // 2
            qkv = norm_matmul(xr, gain, sb_w_qkv[c].astype(BF16), BF16, tm=tm, tn=1024)
            o = sb_attention(qkv.reshape(b, s, -1), tq_sb)
            xr = matmul_resid(o.reshape(rows, d), sb_w_out[c].astype(BF16), xr, tm=tm, tn=d)

        q = norm_matmul(xr, row(norm_xattn[layer]), xa_w_q[layer].astype(BF16), BF16, tm=tm, tn=d)
        o = cross_attention(q.reshape(b, s, d), kv, layer, tq=min(512, s))
        xr = matmul_resid(o.reshape(rows, d), xa_w_o[layer].astype(BF16), xr, tm=tm, tn=d)

        act = ffn_in(xr, row(norm_ffn[layer]), ffn_w_in[layer].astype(BF16), ffn_conv_w[layer],
                     row(ffn_conv_b[layer]), seq=s, tm=tm, tn=512)
        xr = matmul_resid(act, ffn_w_out[layer].astype(BF16), xr, tm=tm, tn=512)

    return rms_norm_rows(xr, row(final_norm), tm).reshape(b, s, d)
```

```python
import functools

import jax
import jax.numpy as jnp
from jax import lax
from jax.experimental import pallas as pl
from jax.experimental.pallas import tpu as pltpu

F32 = jnp.float32
BF16 = jnp.bfloat16

RMS_EPS = 1e-6
CHUNK = 64
HGRN_HEADS = 8
HGRN_HEAD_DIM = 128
GLA_HEADS = 4
GLA_HEAD_K = 128
GLA_HEAD_V = 256
GLA_GATE_RANK = 16
GLA_GATE_NORMALIZER = 16.0
SB_HEADS = 16
SB_HEAD_DIM = 128
XA_HEADS = 4
CONV_W = 3
LOG2_E = 1.4426950408889634
EXP2_CLAMP = 126.0
SB_HEADS_PER_STEP = 4

LANES = 128
BF16_SUBLANES = 16
VMEM_LIMIT_BYTES = 52 * 1024 * 1024


def _params(*semantics):
    return pltpu.CompilerParams(dimension_semantics=semantics, vmem_limit_bytes=VMEM_LIMIT_BYTES)


def _rms(x, gain):
    return x * lax.rsqrt(jnp.mean(x * x, axis=-1, keepdims=True) + RMS_EPS) * gain


def _sigmoid(x):
    return 1.0 / (1.0 + jnp.exp(-x))


def _softplus(x):
    return jnp.maximum(x, 0.0) + jnp.log(1.0 + jnp.exp(-jnp.abs(x)))


def _dot(a, b):
    return jnp.dot(a, b, preferred_element_type=F32)


def _dot_nt(a, b):
    return lax.dot_general(a, b, (((1,), (1,)), ((), ())), preferred_element_type=F32)


def _dot_tn(a, b):
    return lax.dot_general(a, b, (((0,), (0,)), ((), ())), preferred_element_type=F32)


def _split3(x):
    hi = x.astype(BF16)
    r1 = x - hi.astype(F32)
    mid = r1.astype(BF16)
    lo = (r1 - mid.astype(F32)).astype(BF16)
    return hi, mid, lo


def _norm_matmul_kernel(x_ref, g_ref, w_ref, o_ref, h_ref, *, lead_blocks, lead_scale):
    @pl.when(pl.program_id(1) == 0)
    def _():
        h_ref[...] = _rms(x_ref[...], g_ref[...]).astype(BF16)

    acc = _dot(h_ref[...], w_ref[...])
    if lead_blocks:
        acc = acc * jnp.where(pl.program_id(1) < lead_blocks, lead_scale, 1.0)
    o_ref[...] = acc.astype(o_ref.dtype)


def norm_matmul(x, gain, w, out_dtype, tm, tn, lead_cols=0, lead_scale=1.0):
    m, k = x.shape
    n = w.shape[1]
    assert lead_cols % tn == 0
    return pl.pallas_call(
        functools.partial(_norm_matmul_kernel, lead_blocks=lead_cols // tn, lead_scale=lead_scale),
        grid=(m // tm, n // tn),
        in_specs=[
            pl.BlockSpec((tm, k), lambda i, j: (i, 0)),
            pl.BlockSpec((1, k), lambda i, j: (0, 0)),
            pl.BlockSpec((k, tn), lambda i, j: (0, j)),
        ],
        out_specs=pl.BlockSpec((tm, tn), lambda i, j: (i, j)),
        out_shape=jax.ShapeDtypeStruct((m, n), out_dtype),
        scratch_shapes=[pltpu.VMEM((tm, k), BF16)],
        compiler_params=_params("parallel", "arbitrary"),
        name="norm_matmul",
    )(x, gain, w)


def _ab_proj_kernel(x_ref, g_ref, w_ref, wlow_ref, wgk_ref, bgk_ref, o_ref, gk_ref, h_ref):
    @pl.when(pl.program_id(1) == 0)
    def _():
        h = _rms(x_ref[...], g_ref[...]).astype(BF16)
        h_ref[...] = h
        low = _dot(h, wlow_ref[...])
        pre = _dot(low.astype(BF16), wgk_ref[...]) + bgk_ref[...]
        gk_ref[...] = -_softplus(-pre) * (1.0 / GLA_GATE_NORMALIZER)

    o_ref[...] = _dot(h_ref[...], w_ref[...])


def ab_proj(x, gain, w, w_low, w_gk, b_gk, tm, tn):
    m, k = x.shape
    n = w.shape[1]
    ngk = w_gk.shape[1]
    return pl.pallas_call(
        _ab_proj_kernel,
        grid=(m // tm, n // tn),
        in_specs=[
            pl.BlockSpec((tm, k), lambda i, j: (i, 0)),
            pl.BlockSpec((1, k), lambda i, j: (0, 0)),
            pl.BlockSpec((k, tn), lambda i, j: (0, j)),
            pl.BlockSpec((k, LANES), lambda i, j: (0, 0)),
            pl.BlockSpec((LANES, ngk), lambda i, j: (0, 0)),
            pl.BlockSpec((1, ngk), lambda i, j: (0, 0)),
        ],
        out_specs=[
            pl.BlockSpec((tm, tn), lambda i, j: (i, j)),
            pl.BlockSpec((tm, ngk), lambda i, j: (i, 0)),
        ],
        out_shape=[
            jax.ShapeDtypeStruct((m, n), F32),
            jax.ShapeDtypeStruct((m, ngk), F32),
        ],
        scratch_shapes=[pltpu.VMEM((tm, k), BF16)],
        compiler_params=_params("parallel", "arbitrary"),
        name="ab_proj",
    )(x, gain, w, w_low, w_gk, b_gk)


def _matmul_resid_kernel(a_ref, w_ref, r_ref, o_ref):
    o_ref[...] = r_ref[...] + _dot(a_ref[...], w_ref[...])


def matmul_resid(a, w, resid, tm, tn):
    m, k = a.shape
    n = w.shape[1]
    return pl.pallas_call(
        _matmul_resid_kernel,
        grid=(m // tm, n // tn),
        in_specs=[
            pl.BlockSpec((tm, k), lambda i, j: (i, 0)),
            pl.BlockSpec((k, tn), lambda i, j: (0, j)),
            pl.BlockSpec((tm, tn), lambda i, j: (i, j)),
        ],
        out_specs=pl.BlockSpec((tm, tn), lambda i, j: (i, j)),
        out_shape=jax.ShapeDtypeStruct((m, n), F32),
        compiler_params=_params("parallel", "arbitrary"),
        name="matmul_resid",
    )(a, w, resid)


def _matmul2_resid_kernel(a1_ref, a2_ref, w1_ref, w2_ref, r_ref, o_ref):
    o_ref[...] = r_ref[...] + (_dot(a1_ref[...], w1_ref[...]) + _dot(a2_ref[...], w2_ref[...]))


def matmul2_resid(a1, a2, w, resid, tm, tn):
    m, kh = a1.shape
    n = w.shape[1]
    return pl.pallas_call(
        _matmul2_resid_kernel,
        grid=(m // tm, n // tn),
        in_specs=[
            pl.BlockSpec((tm, kh), lambda i, j: (i, 0)),
            pl.BlockSpec((tm, kh), lambda i, j: (i, 0)),
            pl.BlockSpec((kh, tn), lambda i, j: (0, j)),
            pl.BlockSpec((kh, tn), lambda i, j: (1, j)),
            pl.BlockSpec((tm, tn), lambda i, j: (i, j)),
        ],
        out_specs=pl.BlockSpec((tm, tn), lambda i, j: (i, j)),
        out_shape=jax.ShapeDtypeStruct((m, n), F32),
        compiler_params=_params("parallel", "arbitrary"),
        name="matmul2_resid",
    )(a1, a2, w, w, resid)


def _linear_attention_block(q, k, v, g, state_ref):
    n = q.shape[0] // CHUNK
    row = lax.broadcasted_iota(jnp.int32, (CHUNK, CHUNK), 0)
    col = lax.broadcasted_iota(jnp.int32, (CHUNK, CHUNK), 1)
    causal = row >= col
    tri = causal.astype(BF16)
    sl = [slice(c * CHUNK, (c + 1) * CHUNK) for c in range(n)]

    g_terms = [_split3(g[s]) for s in sl]
    b = [_dot(tri, hi) + _dot(tri, mid) + _dot(tri, lo) for hi, mid, lo in g_terms]
    b_mid = [x[CHUNK // 2 - 1:CHUNK // 2] for x in b]
    b_last = [x[CHUNK - 1:] for x in b]
    vb = [v[s].astype(BF16) for s in sl]
    qs = [(q[s] * jnp.exp(x - m)).astype(BF16) for s, x, m in zip(sl, b, b_mid)]
    ks = [(k[s] * jnp.exp(m - x)).astype(BF16) for s, x, m in zip(sl, b, b_mid)]
    kd = [(k[s] * jnp.exp(l - x)).astype(BF16) for s, x, l in zip(sl, b, b_last)]
    qd = [(q[s] * jnp.exp(x)).astype(BF16) for s, x in zip(sl, b)]

    scores = [jnp.where(causal, _dot_nt(a, c), 0.0).astype(BF16) for a, c in zip(qs, ks)]
    update = [_dot_tn(a, c) for a, c in zip(vb, kd)]
    o_intra = [_dot(a, c) for a, c in zip(scores, vb)]

    state = state_ref[...]
    states = []
    for c in range(n):
        states.append(state.astype(BF16))
        state = state * jnp.exp(b_last[c]) + update[c]
    state_ref[...] = state

    o = [oi + _dot_nt(a, st) for oi, a, st in zip(o_intra, qd, states)]
    return jnp.concatenate(o, axis=0)


def _hgrn_kernel(aq_ref, af_ref, ai_ref, ag_ref, lbl_ref, gain_ref, o_ref, state_ref, *, lb_rows):
    @pl.when(pl.program_id(2) == 0)
    def _():
        state_ref[...] = jnp.zeros_like(state_ref)

    logits = lbl_ref[...]
    e = jnp.exp(logits - jnp.max(logits, axis=0, keepdims=True))
    lb = jnp.sum(e[:lb_rows], axis=0, keepdims=True) / jnp.sum(e, axis=0, keepdims=True)

    a_q = aq_ref[0]
    f = lb + (1.0 - lb) * _sigmoid(af_ref[0])
    o = _linear_attention_block(a_q * _sigmoid(a_q), 1.0 - f, ai_ref[0], jnp.log(f), state_ref)
    o_ref[0] = (_rms(o, gain_ref[...]) * _sigmoid(ag_ref[0])).astype(o_ref.dtype)


def hgrn_mixer(p, lb_logits, a_idx, gain, ts):
    b, s, _ = p.shape
    nh, hd = HGRN_HEADS, HGRN_HEAD_DIM
    nlb = lb_logits.shape[0]

    def col(part):
        return pl.BlockSpec((1, ts, hd), lambda bi, h, si: (bi, si, part * nh + h))

    return pl.pallas_call(
        functools.partial(_hgrn_kernel, lb_rows=a_idx + 1),
        grid=(b, nh, s // ts),
        in_specs=[
            col(0), col(1), col(2), col(3),
            pl.BlockSpec((nlb, hd), lambda bi, h, si: (0, h)),
            pl.BlockSpec((1, hd), lambda bi, h, si: (0, 0)),
        ],
        out_specs=pl.BlockSpec((1, ts, hd), lambda bi, h, si: (bi, si, h)),
        out_shape=jax.ShapeDtypeStruct((b, s, nh * hd), BF16),
        scratch_shapes=[pltpu.VMEM((hd, hd), F32)],
        compiler_params=_params("parallel", "parallel", "arbitrary"),
        name="hgrn_mixer",
    )(p, p, p, p, lb_logits, gain)


def _gla_kernel(q_ref, k_ref, v_ref, gg_ref, gk_ref, gain_ref, o_ref, state_ref):
    @pl.when(pl.program_id(2) == 0)
    def _():
        state_ref[...] = jnp.zeros_like(state_ref)

    q = q_ref[0] * (GLA_HEAD_K ** -0.5)
    o = _linear_attention_block(q, k_ref[0], v_ref[0], gk_ref[0], state_ref)
    g_g = gg_ref[0]
    o_ref[0] = (_rms(o, gain_ref[...]) * (g_g * _sigmoid(g_g))).astype(o_ref.dtype)


def gla_mixer(p, gk, gain, ts):
    b, s, _ = p.shape
    nh, kd, vd = GLA_HEADS, GLA_HEAD_K, GLA_HEAD_V
    base = 4 * HGRN_HEADS * HGRN_HEAD_DIM
    q0 = base // kd
    k0 = q0 + nh
    v0 = (base + 2 * nh * kd) // vd
    g0 = v0 + nh
    return pl.pallas_call(
        _gla_kernel,
        grid=(b, nh, s // ts),
        in_specs=[
            pl.BlockSpec((1, ts, kd), lambda bi, h, si: (bi, si, q0 + h)),
            pl.BlockSpec((1, ts, kd), lambda bi, h, si: (bi, si, k0 + h)),
            pl.BlockSpec((1, ts, vd), lambda bi, h, si: (bi, si, v0 + h)),
            pl.BlockSpec((1, ts, vd), lambda bi, h, si: (bi, si, g0 + h)),
            pl.BlockSpec((1, ts, kd), lambda bi, h, si: (bi, si, h)),
            pl.BlockSpec((1, vd), lambda bi, h, si: (0, 0)),
        ],
        out_specs=pl.BlockSpec((1, ts, vd), lambda bi, h, si: (bi, si, h)),
        out_shape=jax.ShapeDtypeStruct((b, s, nh * vd), BF16),
        scratch_shapes=[pltpu.VMEM((vd, kd), F32)],
        compiler_params=_params("parallel", "parallel", "arbitrary"),
        name="gla_mixer",
    )(p, p, p, p, gk, gain)


def _sb_kernel(q_ref, k_ref, v_ref, o_ref, *, tq, heads):
    qi = pl.program_id(2)
    hd = SB_HEAD_DIM
    row = lax.broadcasted_iota(jnp.int32, (2 * tq, tq), 0)
    col = lax.broadcasted_iota(jnp.int32, (2 * tq, tq), 1)
    suffix_ones = (jnp.where(row >= tq, row - tq, row) >= col).astype(BF16)
    strictly_causal = (lax.broadcasted_iota(jnp.int32, (tq, tq), 1)
                       < lax.broadcasted_iota(jnp.int32, (tq, tq), 0))
    head_cols = [slice(h * hd, (h + 1) * hd) for h in range(heads)]

    def key_block(j, carry, diagonal):
        start = pl.multiple_of(j * tq, tq)
        acc = [c[0] for c in carry]
        later = [c[1] for c in carry]
        z2 = [_dot_nt(q_ref[0, :, hc], k_ref[0, pl.ds(start, tq), hc]) for hc in head_cols]
        sp2 = [jnp.maximum(z, jnp.log(1.0 + jnp.exp2(jnp.minimum(z, EXP2_CLAMP))) * LOG2_E)
               for z in z2]
        if diagonal:
            sp2 = [jnp.where(strictly_causal, x, 0.0) for x in sp2]
        sp_hi = [x.astype(BF16) for x in sp2]
        sp_lo = [(x - hi.astype(F32)).astype(BF16) for x, hi in zip(sp2, sp_hi)]
        cum2 = [_dot(jnp.concatenate([hi, lo], axis=1), suffix_ones) + lt
                for hi, lo, lt in zip(sp_hi, sp_lo, later)]
        log2_a = [z - c for z, c in zip(z2, cum2)]
        if diagonal:
            log2_a = [jnp.where(strictly_causal, x, -jnp.inf) for x in log2_a]
        att = [jnp.exp2(x).astype(BF16) for x in log2_a]
        acc = [a + _dot(p, v_ref[0, pl.ds(start, tq), hc]) for a, p, hc in zip(acc, att, head_cols)]
        return tuple((a, c[:, 0:1]) for a, c in zip(acc, cum2))

    zero = (jnp.zeros((tq, hd), F32), jnp.zeros((tq, 1), F32))
    carry = key_block(qi, (zero,) * heads, True)
    carry = lax.fori_loop(0, qi, lambda t, c: key_block(qi - 1 - t, c, False), carry)
    for h, hc in enumerate(head_cols):
        o_ref[0, :, hc] = carry[h][0].astype(o_ref.dtype)


def sb_attention(qkv, tq, heads):
    b, s, _ = qkv.shape
    ng = SB_HEADS // heads
    w = heads * SB_HEAD_DIM
    return pl.pallas_call(
        functools.partial(_sb_kernel, tq=tq, heads=heads),
        grid=(b, ng, s // tq),
        in_specs=[
            pl.BlockSpec((1, tq, w), lambda bi, g, qi: (bi, qi, g)),
            pl.BlockSpec((1, s, w), lambda bi, g, qi: (bi, 0, ng + g)),
            pl.BlockSpec((1, s, w), lambda bi, g, qi: (bi, 0, 2 * ng + g)),
        ],
        out_specs=pl.BlockSpec((1, tq, w), lambda bi, g, qi: (bi, qi, g)),
        out_shape=jax.ShapeDtypeStruct((b, s, SB_HEADS * SB_HEAD_DIM), BF16),
        compiler_params=_params("parallel", "parallel", "arbitrary"),
        name="sb_attention",
    )(qkv, qkv, qkv)


def _xattn_kernel(q_ref, k_ref, v_ref, o_ref):
    q = q_ref[0]
    scores = _dot_nt(q, k_ref[0]) * (q.shape[-1] ** -0.5)
    e = jnp.exp(scores - jnp.max(scores, axis=-1, keepdims=True))
    probs = e / jnp.sum(e, axis=-1, keepdims=True)
    o_ref[0] = _dot(probs.astype(BF16), v_ref[0]).astype(o_ref.dtype)


def cross_attention(q, kv, layer, tq):
    b, s, d = q.shape
    m = kv.shape[1]
    nh = XA_HEADS
    hd = d // nh
    k0 = layer * 2 * nh
    return pl.pallas_call(
        _xattn_kernel,
        grid=(b, nh, s // tq),
        in_specs=[
            pl.BlockSpec((1, tq, hd), lambda bi, h, si: (bi, si, h)),
            pl.BlockSpec((1, m, hd), lambda bi, h, si: (bi, 0, k0 + h)),
            pl.BlockSpec((1, m, hd), lambda bi, h, si: (bi, 0, k0 + nh + h)),
        ],
        out_specs=pl.BlockSpec((1, tq, hd), lambda bi, h, si: (bi, si, h)),
        out_shape=jax.ShapeDtypeStruct((b, s, d), BF16),
        compiler_params=_params("parallel", "parallel", "arbitrary"),
        name="cross_attention",
    )(q, kv, kv)


HALO = BF16_SUBLANES


def _ffn_in_kernel(x_ref, halo_ref, g_ref, wu_ref, wg_ref, cw_ref, cb_ref, o_ref, h_ref, gs_ref,
                   *, tm, sub, blocks_per_seq):
    inside = pl.program_id(0) % blocks_per_seq != 0

    @pl.when(pl.program_id(1) == 0)
    def _():
        h_ref[HALO:, :] = _rms(x_ref[...], g_ref[...]).astype(BF16)
        hh = _rms(halo_ref[...], g_ref[...])
        h_ref[:HALO, :] = jnp.where(inside, hh, 0.0).astype(BF16)

    cw = cw_ref[...]
    cb = cb_ref[...]
    for r in range(tm // sub):
        lo = HALO + r * sub
        first = 0 if r == 0 else lo
        gs_ref[first:lo + sub, :] = _dot(h_ref[first:lo + sub, :], wg_ref[...])
        u = _dot(h_ref[lo:lo + sub, :], wu_ref[...])
        conv = 0.0
        for tap in range(CONV_W):
            t0 = lo - (CONV_W - 1) + tap
            conv = conv + cw[tap:tap + 1] * gs_ref[t0:t0 + sub, :]
        gc = cb + conv
        o_ref[r * sub:(r + 1) * sub, :] = (gc * _sigmoid(gc) * u).astype(o_ref.dtype)


def ffn_in(x, gain, w_in, conv_w, conv_b, seq, tm, tn, sub):
    m, d = x.shape
    f = w_in.shape[1] // 2
    nj = f // tn
    halo_per_block = tm // HALO
    return pl.pallas_call(
        functools.partial(_ffn_in_kernel, tm=tm, sub=sub, blocks_per_seq=seq // tm),
        grid=(m // tm, nj),
        in_specs=[
            pl.BlockSpec((tm, d), lambda i, j: (i, 0)),
            pl.BlockSpec((HALO, d), lambda i, j: (jnp.maximum(i * halo_per_block - 1, 0), 0)),
            pl.BlockSpec((1, d), lambda i, j: (0, 0)),
            pl.BlockSpec((d, tn), lambda i, j: (0, j)),
            pl.BlockSpec((d, tn), lambda i, j: (0, nj + j)),
            pl.BlockSpec((CONV_W, tn), lambda i, j: (0, j)),
            pl.BlockSpec((1, tn), lambda i, j: (0, j)),
        ],
        out_specs=pl.BlockSpec((tm, tn), lambda i, j: (i, j)),
        out_shape=jax.ShapeDtypeStruct((m, f), BF16),
        scratch_shapes=[pltpu.VMEM((HALO + tm, d), BF16), pltpu.VMEM((HALO + tm, tn), F32)],
        compiler_params=_params("parallel", "arbitrary"),
        name="ffn_in",
    )(x, x, gain, w_in, w_in, conv_w, conv_b)


def _norm_kernel(x_ref, g_ref, o_ref):
    o_ref[...] = _rms(x_ref[...], g_ref[...])


def rms_norm_rows(x, gain, tm):
    m, d = x.shape
    return pl.pallas_call(
        _norm_kernel,
        grid=(m // tm,),
        in_specs=[pl.BlockSpec((tm, d), lambda i: (i, 0)), pl.BlockSpec((1, d), lambda i: (0, 0))],
        out_specs=pl.BlockSpec((tm, d), lambda i: (i, 0)),
        out_shape=jax.ShapeDtypeStruct((m, d), F32),
        compiler_params=_params("parallel"),
        name="final_norm",
    )(x, gain)


def kernel(x, mem, mem_norm, norm_mix, norm_xattn, norm_ffn, ab_w_in, hgrn_lb_logits, hgrn_norm, gla_w_gk, gla_b_gk, gla_norm, ab_w_out, sb_w_qkv, sb_w_out, xa_w_q, xa_w_kv, xa_w_o, ffn_w_in, ffn_conv_w, ffn_conv_b, ffn_w_out, final_norm):
    b, s, d = x.shape
    depth = norm_mix.shape[0]
    m_len = mem.shape[1]
    rows = b * s
    tm = min(512, s)
    ts = min(512, s)
    tq_sb = min(256, s)

    def row(v):
        return v.reshape(1, -1)

    xr = x.reshape(rows, d)

    w_kv_all = jnp.concatenate([xa_w_kv[l] for l in range(depth)], axis=1).astype(BF16)
    kv = norm_matmul(mem.reshape(b * m_len, d), row(mem_norm), w_kv_all, BF16,
                     tm=min(512, b * m_len), tn=1024)
    kv = kv.reshape(b, m_len, -1)

    ab_main = 4 * HGRN_HEADS * HGRN_HEAD_DIM + GLA_HEADS * (2 * GLA_HEAD_K + 2 * GLA_HEAD_V)

    for layer in range(depth):
        gain = row(norm_mix[layer])
        if layer % 2 == 0:
            a = layer // 2
            w_in = ab_w_in[a]
            w_low = jnp.pad(w_in[:, ab_main:], ((0, 0), (0, LANES - GLA_GATE_RANK))).astype(BF16)
            w_gk = jnp.pad(gla_w_gk[a], ((0, LANES - GLA_GATE_RANK), (0, 0))).astype(BF16)
            p, gk = ab_proj(xr, gain, w_in[:, :ab_main].astype(BF16), w_low, w_gk,
                            row(gla_b_gk[a]), tm=tm, tn=1024)
            p = p.reshape(b, s, ab_main)
            o_a = hgrn_mixer(p, hgrn_lb_logits, a, row(hgrn_norm[a]), ts)
            o_b = gla_mixer(p, gk.reshape(b, s, -1), row(gla_norm[a]), ts)
            xr = matmul2_resid(o_a.reshape(rows, -1), o_b.reshape(rows, -1),
                               ab_w_out[a].astype(BF16), xr, tm=tm, tn=d)
        else:
            c = layer // 2
            qkv = norm_matmul(xr, gain, sb_w_qkv[c].astype(BF16), BF16, tm=tm, tn=1024,
                              lead_cols=d, lead_scale=(SB_HEAD_DIM ** -0.5) * LOG2_E)
            o = sb_attention(qkv.reshape(b, s, -1), tq_sb, heads=SB_HEADS_PER_STEP)
            xr = matmul_resid(o.reshape(rows, d), sb_w_out[c].astype(BF16), xr, tm=tm, tn=d)

        q = norm_matmul(xr, row(norm_xattn[layer]), xa_w_q[layer].astype(BF16), BF16, tm=tm, tn=d)
        o = cross_attention(q.reshape(b, s, d), kv, layer, tq=min(512, s))
        xr = matmul_resid(o.reshape(rows, d), xa_w_o[layer].astype(BF16), xr, tm=tm, tn=d)

        act = ffn_in(xr, row(norm_ffn[layer]), ffn_w_in[layer].astype(BF16), ffn_conv_w[layer],
                     row(ffn_conv_b[layer]), seq=s, tm=min(1024, s), tn=512, sub=256)
        xr = matmul_resid(act, ffn_w_out[layer].astype(BF16), xr, tm=tm, tn=512)

    return rms_norm_rows(xr, row(final_norm), tm).reshape(b, s, d)
```

```python
import functools

import jax
import jax.numpy as jnp
from jax import lax
from jax.experimental import pallas as pl
from jax.experimental.pallas import tpu as pltpu

F32 = jnp.float32
BF16 = jnp.bfloat16

RMS_EPS = 1e-6
CHUNK = 64
HGRN_HEADS = 8
HGRN_HEAD_DIM = 128
GLA_HEADS = 4
GLA_HEAD_K = 128
GLA_HEAD_V = 256
GLA_GATE_RANK = 16
GLA_GATE_NORMALIZER = 16.0
SB_HEADS = 16
SB_HEAD_DIM = 128
XA_HEADS = 4
CONV_W = 3
LOG2_E = 1.4426950408889634
EXP2_CLAMP = 126.0
SB_HEADS_PER_STEP = 4

LANES = 128
BF16_SUBLANES = 16
VMEM_LIMIT_BYTES = 52 * 1024 * 1024


def _params(*semantics):
    return pltpu.CompilerParams(dimension_semantics=semantics, vmem_limit_bytes=VMEM_LIMIT_BYTES)


def _rms(x, gain):
    return x * lax.rsqrt(jnp.mean(x * x, axis=-1, keepdims=True) + RMS_EPS) * gain


def _sigmoid(x):
    return 1.0 / (1.0 + jnp.exp(-x))


def _softplus(x):
    return jnp.maximum(x, 0.0) + jnp.log(1.0 + jnp.exp(-jnp.abs(x)))


def _dot(a, b):
    return jnp.dot(a, b, preferred_element_type=F32)


def _dot_nt(a, b):
    return lax.dot_general(a, b, (((1,), (1,)), ((), ())), preferred_element_type=F32)


def _dot_tn(a, b):
    return lax.dot_general(a, b, (((0,), (0,)), ((), ())), preferred_element_type=F32)


def _split3(x):
    hi = x.astype(BF16)
    r1 = x - hi.astype(F32)
    mid = r1.astype(BF16)
    lo = (r1 - mid.astype(F32)).astype(BF16)
    return hi, mid, lo


def _norm_matmul_kernel(x_ref, g_ref, w_ref, o_ref, h_ref, *, lead_blocks, lead_scale):
    @pl.when(pl.program_id(1) == 0)
    def _():
        h_ref[...] = _rms(x_ref[...], g_ref[...]).astype(BF16)

    acc = _dot(h_ref[...], w_ref[...])
    if lead_blocks:
        acc = acc * jnp.where(pl.program_id(1) < lead_blocks, lead_scale, 1.0)
    o_ref[...] = acc.astype(o_ref.dtype)


def norm_matmul(x, gain, w, out_dtype, tm, tn, lead_cols=0, lead_scale=1.0):
    m, k = x.shape
    n = w.shape[1]
    assert lead_cols % tn == 0
    return pl.pallas_call(
        functools.partial(_norm_matmul_kernel, lead_blocks=lead_cols // tn, lead_scale=lead_scale),
        grid=(m // tm, n // tn),
        in_specs=[
            pl.BlockSpec((tm, k), lambda i, j: (i, 0)),
            pl.BlockSpec((1, k), lambda i, j: (0, 0)),
            pl.BlockSpec((k, tn), lambda i, j: (0, j)),
        ],
        out_specs=pl.BlockSpec((tm, tn), lambda i, j: (i, j)),
        out_shape=jax.ShapeDtypeStruct((m, n), out_dtype),
        scratch_shapes=[pltpu.VMEM((tm, k), BF16)],
        compiler_params=_params("parallel", "arbitrary"),
        name="norm_matmul",
    )(x, gain, w)


def _ab_proj_kernel(x_ref, g_ref, w_ref, wlow_ref, wgk_ref, bgk_ref, o_ref, gk_ref, h_ref):
    @pl.when(pl.program_id(1) == 0)
    def _():
        h = _rms(x_ref[...], g_ref[...]).astype(BF16)
        h_ref[...] = h
        low = _dot(h, wlow_ref[...])
        pre = _dot(low.astype(BF16), wgk_ref[...]) + bgk_ref[...]
        gk_ref[...] = -_softplus(-pre) * (1.0 / GLA_GATE_NORMALIZER)

    o_ref[...] = _dot(h_ref[...], w_ref[...])


def ab_proj(x, gain, w, n, w_low, w_gk, b_gk, tm, tn):
    m, k = x.shape
    assert n % tn == 0
    ngk = w_gk.shape[1]
    return pl.pallas_call(
        _ab_proj_kernel,
        grid=(m // tm, n // tn),
        in_specs=[
            pl.BlockSpec((tm, k), lambda i, j: (i, 0)),
            pl.BlockSpec((1, k), lambda i, j: (0, 0)),
            pl.BlockSpec((k, tn), lambda i, j: (0, j)),
            pl.BlockSpec((k, LANES), lambda i, j: (0, 0)),
            pl.BlockSpec((LANES, ngk), lambda i, j: (0, 0)),
            pl.BlockSpec((1, ngk), lambda i, j: (0, 0)),
        ],
        out_specs=[
            pl.BlockSpec((tm, tn), lambda i, j: (i, j)),
            pl.BlockSpec((tm, ngk), lambda i, j: (i, 0)),
        ],
        out_shape=[
            jax.ShapeDtypeStruct((m, n), F32),
            jax.ShapeDtypeStruct((m, ngk), F32),
        ],
        scratch_shapes=[pltpu.VMEM((tm, k), BF16)],
        compiler_params=_params("parallel", "arbitrary"),
        name="ab_proj",
    )(x, gain, w, w_low, w_gk, b_gk)


def _matmul_resid_kernel(a_ref, w_ref, r_ref, o_ref):
    o_ref[...] = r_ref[...] + _dot(a_ref[...], w_ref[...])


def matmul_resid(a, w, resid, tm, tn):
    m, k = a.shape
    n = w.shape[1]
    return pl.pallas_call(
        _matmul_resid_kernel,
        grid=(m // tm, n // tn),
        in_specs=[
            pl.BlockSpec((tm, k), lambda i, j: (i, 0)),
            pl.BlockSpec((k, tn), lambda i, j: (0, j)),
            pl.BlockSpec((tm, tn), lambda i, j: (i, j)),
        ],
        out_specs=pl.BlockSpec((tm, tn), lambda i, j: (i, j)),
        out_shape=jax.ShapeDtypeStruct((m, n), F32),
        compiler_params=_params("parallel", "arbitrary"),
        name="matmul_resid",
    )(a, w, resid)


def _matmul2_resid_kernel(a1_ref, a2_ref, w1_ref, w2_ref, r_ref, o_ref):
    o_ref[...] = r_ref[...] + (_dot(a1_ref[...], w1_ref[...]) + _dot(a2_ref[...], w2_ref[...]))


def matmul2_resid(a1, a2, w, resid, tm, tn):
    m, kh = a1.shape
    n = w.shape[1]
    return pl.pallas_call(
        _matmul2_resid_kernel,
        grid=(m // tm, n // tn),
        in_specs=[
            pl.BlockSpec((tm, kh), lambda i, j: (i, 0)),
            pl.BlockSpec((tm, kh), lambda i, j: (i, 0)),
            pl.BlockSpec((kh, tn), lambda i, j: (0, j)),
            pl.BlockSpec((kh, tn), lambda i, j: (1, j)),
            pl.BlockSpec((tm, tn), lambda i, j: (i, j)),
        ],
        out_specs=pl.BlockSpec((tm, tn), lambda i, j: (i, j)),
        out_shape=jax.ShapeDtypeStruct((m, n), F32),
        compiler_params=_params("parallel", "arbitrary"),
        name="matmul2_resid",
    )(a1, a2, w, w, resid)


def _linear_attention_block(q, k, v, g, state_ref):
    n = q.shape[0] // CHUNK
    row = lax.broadcasted_iota(jnp.int32, (CHUNK, CHUNK), 0)
    col = lax.broadcasted_iota(jnp.int32, (CHUNK, CHUNK), 1)
    causal = row >= col
    tri = causal.astype(BF16)
    sl = [slice(c * CHUNK, (c + 1) * CHUNK) for c in range(n)]

    g_terms = [_split3(g[s]) for s in sl]
    b = [_dot(tri, hi) + _dot(tri, mid) + _dot(tri, lo) for hi, mid, lo in g_terms]
    b_mid = [x[CHUNK // 2 - 1:CHUNK // 2] for x in b]
    b_last = [x[CHUNK - 1:] for x in b]
    vb = [v[s].astype(BF16) for s in sl]
    qs = [(q[s] * jnp.exp(x - m)).astype(BF16) for s, x, m in zip(sl, b, b_mid)]
    ks = [(k[s] * jnp.exp(m - x)).astype(BF16) for s, x, m in zip(sl, b, b_mid)]
    kd = [(k[s] * jnp.exp(l - x)).astype(BF16) for s, x, l in zip(sl, b, b_last)]
    qd = [(q[s] * jnp.exp(x)).astype(BF16) for s, x in zip(sl, b)]

    scores = [jnp.where(causal, _dot_nt(a, c), 0.0).astype(BF16) for a, c in zip(qs, ks)]
    update = [_dot_tn(a, c) for a, c in zip(vb, kd)]
    o_intra = [_dot(a, c) for a, c in zip(scores, vb)]

    state = state_ref[...]
    states = []
    for c in range(n):
        states.append(state.astype(BF16))
        state = state * jnp.exp(b_last[c]) + update[c]
    state_ref[...] = state

    o = [oi + _dot_nt(a, st) for oi, a, st in zip(o_intra, qd, states)]
    return jnp.concatenate(o, axis=0)


def _hgrn_kernel(aq_ref, af_ref, ai_ref, ag_ref, lbl_ref, gain_ref, o_ref, state_ref, *, lb_rows):
    @pl.when(pl.program_id(2) == 0)
    def _():
        state_ref[...] = jnp.zeros_like(state_ref)

    logits = lbl_ref[...]
    e = jnp.exp(logits - jnp.max(logits, axis=0, keepdims=True))
    lb = jnp.sum(e[:lb_rows], axis=0, keepdims=True) / jnp.sum(e, axis=0, keepdims=True)

    a_q = aq_ref[0]
    f = lb + (1.0 - lb) * _sigmoid(af_ref[0])
    o = _linear_attention_block(a_q * _sigmoid(a_q), 1.0 - f, ai_ref[0], jnp.log(f), state_ref)
    o_ref[0] = (_rms(o, gain_ref[...]) * _sigmoid(ag_ref[0])).astype(o_ref.dtype)


def hgrn_mixer(p, lb_logits, a_idx, gain, ts):
    b, s, _ = p.shape
    nh, hd = HGRN_HEADS, HGRN_HEAD_DIM
    nlb = lb_logits.shape[0]

    def col(part):
        return pl.BlockSpec((1, ts, hd), lambda bi, h, si: (bi, si, part * nh + h))

    return pl.pallas_call(
        functools.partial(_hgrn_kernel, lb_rows=a_idx + 1),
        grid=(b, nh, s // ts),
        in_specs=[
            col(0), col(1), col(2), col(3),
            pl.BlockSpec((nlb, hd), lambda bi, h, si: (0, h)),
            pl.BlockSpec((1, hd), lambda bi, h, si: (0, 0)),
        ],
        out_specs=pl.BlockSpec((1, ts, hd), lambda bi, h, si: (bi, si, h)),
        out_shape=jax.ShapeDtypeStruct((b, s, nh * hd), BF16),
        scratch_shapes=[pltpu.VMEM((hd, hd), F32)],
        compiler_params=_params("parallel", "parallel", "arbitrary"),
        name="hgrn_mixer",
    )(p, p, p, p, lb_logits, gain)


def _gla_kernel(q_ref, k_ref, v_ref, gg_ref, gk_ref, gain_ref, o_ref, state_ref):
    @pl.when(pl.program_id(2) == 0)
    def _():
        state_ref[...] = jnp.zeros_like(state_ref)

    q = q_ref[0] * (GLA_HEAD_K ** -0.5)
    o = _linear_attention_block(q, k_ref[0], v_ref[0], gk_ref[0], state_ref)
    g_g = gg_ref[0]
    o_ref[0] = (_rms(o, gain_ref[...]) * (g_g * _sigmoid(g_g))).astype(o_ref.dtype)


def gla_mixer(p, gk, gain, ts):
    b, s, _ = p.shape
    nh, kd, vd = GLA_HEADS, GLA_HEAD_K, GLA_HEAD_V
    base = 4 * HGRN_HEADS * HGRN_HEAD_DIM
    q0 = base // kd
    k0 = q0 + nh
    v0 = (base + 2 * nh * kd) // vd
    g0 = v0 + nh
    return pl.pallas_call(
        _gla_kernel,
        grid=(b, nh, s // ts),
        in_specs=[
            pl.BlockSpec((1, ts, kd), lambda bi, h, si: (bi, si, q0 + h)),
            pl.BlockSpec((1, ts, kd), lambda bi, h, si: (bi, si, k0 + h)),
            pl.BlockSpec((1, ts, vd), lambda bi, h, si: (bi, si, v0 + h)),
            pl.BlockSpec((1, ts, vd), lambda bi, h, si: (bi, si, g0 + h)),
            pl.BlockSpec((1, ts, kd), lambda bi, h, si: (bi, si, h)),
            pl.BlockSpec((1, vd), lambda bi, h, si: (0, 0)),
        ],
        out_specs=pl.BlockSpec((1, ts, vd), lambda bi, h, si: (bi, si, h)),
        out_shape=jax.ShapeDtypeStruct((b, s, nh * vd), BF16),
        scratch_shapes=[pltpu.VMEM((vd, kd), F32)],
        compiler_params=_params("parallel", "parallel", "arbitrary"),
        name="gla_mixer",
    )(p, p, p, p, gk, gain)


def _sb_kernel(q_ref, k_ref, v_ref, o_ref, *, tq, heads):
    qi = pl.program_id(2)
    hd = SB_HEAD_DIM
    row = lax.broadcasted_iota(jnp.int32, (2 * tq, tq), 0)
    col = lax.broadcasted_iota(jnp.int32, (2 * tq, tq), 1)
    suffix_ones = (jnp.where(row >= tq, row - tq, row) >= col).astype(BF16)
    strictly_causal = (lax.broadcasted_iota(jnp.int32, (tq, tq), 1)
                       < lax.broadcasted_iota(jnp.int32, (tq, tq), 0))
    head_cols = [slice(h * hd, (h + 1) * hd) for h in range(heads)]

    def key_block(j, carry, diagonal):
        start = pl.multiple_of(j * tq, tq)
        acc = [c[0] for c in carry]
        later = [c[1] for c in carry]
        z2 = [_dot_nt(q_ref[0, :, hc], k_ref[0, pl.ds(start, tq), hc]) for hc in head_cols]
        sp2 = [jnp.maximum(z, jnp.log(1.0 + jnp.exp2(jnp.minimum(z, EXP2_CLAMP))) * LOG2_E)
               for z in z2]
        if diagonal:
            sp2 = [jnp.where(strictly_causal, x, 0.0) for x in sp2]
        sp_hi = [x.astype(BF16) for x in sp2]
        sp_lo = [(x - hi.astype(F32)).astype(BF16) for x, hi in zip(sp2, sp_hi)]
        cum2 = [_dot(jnp.concatenate([hi, lo], axis=1), suffix_ones) + lt
                for hi, lo, lt in zip(sp_hi, sp_lo, later)]
        log2_a = [z - c for z, c in zip(z2, cum2)]
        if diagonal:
            log2_a = [jnp.where(strictly_causal, x, -jnp.inf) for x in log2_a]
        att = [jnp.exp2(x).astype(BF16) for x in log2_a]
        acc = [a + _dot(p, v_ref[0, pl.ds(start, tq), hc]) for a, p, hc in zip(acc, att, head_cols)]
        return tuple((a, c[:, 0:1]) for a, c in zip(acc, cum2))

    zero = (jnp.zeros((tq, hd), F32), jnp.zeros((tq, 1), F32))
    carry = key_block(qi, (zero,) * heads, True)
    carry = lax.fori_loop(0, qi, lambda t, c: key_block(qi - 1 - t, c, False), carry)
    for h, hc in enumerate(head_cols):
        o_ref[0, :, hc] = carry[h][0].astype(o_ref.dtype)


def sb_attention(qkv, tq, heads):
    b, s, _ = qkv.shape
    ng = SB_HEADS // heads
    w = heads * SB_HEAD_DIM
    return pl.pallas_call(
        functools.partial(_sb_kernel, tq=tq, heads=heads),
        grid=(b, ng, s // tq),
        in_specs=[
            pl.BlockSpec((1, tq, w), lambda bi, g, qi: (bi, qi, g)),
            pl.BlockSpec((1, s, w), lambda bi, g, qi: (bi, 0, ng + g)),
            pl.BlockSpec((1, s, w), lambda bi, g, qi: (bi, 0, 2 * ng + g)),
        ],
        out_specs=pl.BlockSpec((1, tq, w), lambda bi, g, qi: (bi, qi, g)),
        out_shape=jax.ShapeDtypeStruct((b, s, SB_HEADS * SB_HEAD_DIM), BF16),
        compiler_params=_params("parallel", "parallel", "arbitrary"),
        name="sb_attention",
    )(qkv, qkv, qkv)


def _xattn_layer_kernel(x_ref, g_ref, wq_ref, k_ref, v_ref, wo_ref, o_ref):
    x = x_ref[...]
    d = x.shape[-1]
    hd = d // XA_HEADS
    head_cols = [slice(h * hd, (h + 1) * hd) for h in range(XA_HEADS)]
    h_in = _rms(x, g_ref[...]).astype(BF16)
    q = [_dot(h_in, wq_ref[:, hc]).astype(BF16) for hc in head_cols]
    scores = [_dot_nt(qh, k_ref[0, :, hc]) * (hd ** -0.5) for qh, hc in zip(q, head_cols)]
    e = [jnp.exp(sc - jnp.max(sc, axis=-1, keepdims=True)) for sc in scores]
    probs = [(eh / jnp.sum(eh, axis=-1, keepdims=True)).astype(BF16) for eh in e]
    o = [_dot(ph, v_ref[0, :, hc]).astype(BF16) for ph, hc in zip(probs, head_cols)]
    o_ref[...] = x + _dot(jnp.concatenate(o, axis=1), wo_ref[...])


def xattn_layer(x, gain, w_q, kv, w_o, layer, seq, tm):
    rows, d = x.shape
    m = kv.shape[1]
    blocks_per_seq = seq // tm
    whole = functools.partial(pl.BlockSpec, (d, d), lambda i: (0, 0), pipeline_mode=pl.Buffered(1))
    return pl.pallas_call(
        _xattn_layer_kernel,
        grid=(rows // tm,),
        in_specs=[
            pl.BlockSpec((tm, d), lambda i: (i, 0)),
            pl.BlockSpec((1, d), lambda i: (0, 0)),
            whole(),
            pl.BlockSpec((1, m, d), lambda i: (i // blocks_per_seq, 0, 2 * layer)),
            pl.BlockSpec((1, m, d), lambda i: (i // blocks_per_seq, 0, 2 * layer + 1)),
            whole(),
        ],
        out_specs=pl.BlockSpec((tm, d), lambda i: (i, 0)),
        out_shape=jax.ShapeDtypeStruct((rows, d), F32),
        compiler_params=_params("parallel"),
        name="xattn_layer",
    )(x, gain, w_q, kv, kv, w_o)


HALO = BF16_SUBLANES


def _ffn_in_kernel(x_ref, halo_ref, g_ref, wu_ref, wg_ref, cw_ref, cb_ref, o_ref, h_ref, gs_ref,
                   *, tm, sub, blocks_per_seq):
    inside = pl.program_id(0) % blocks_per_seq != 0

    @pl.when(pl.program_id(1) == 0)
    def _():
        h_ref[HALO:, :] = _rms(x_ref[...], g_ref[...]).astype(BF16)
        hh = _rms(halo_ref[...], g_ref[...])
        h_ref[:HALO, :] = jnp.where(inside, hh, 0.0).astype(BF16)

    cw = cw_ref[...]
    cb = cb_ref[...]
    for r in range(tm // sub):
        lo = HALO + r * sub
        first = 0 if r == 0 else lo
        gs_ref[first:lo + sub, :] = _dot(h_ref[first:lo + sub, :], wg_ref[...])
        u = _dot(h_ref[lo:lo + sub, :], wu_ref[...])
        conv = 0.0
        for tap in range(CONV_W):
            t0 = lo - (CONV_W - 1) + tap
            conv = conv + cw[tap:tap + 1] * gs_ref[t0:t0 + sub, :]
        gc = cb + conv
        o_ref[r * sub:(r + 1) * sub, :] = (gc * _sigmoid(gc) * u).astype(o_ref.dtype)


def ffn_in(x, gain, w_in, conv_w, conv_b, seq, tm, tn, sub):
    m, d = x.shape
    f = w_in.shape[1] // 2
    nj = f // tn
    halo_per_block = tm // HALO
    return pl.pallas_call(
        functools.partial(_ffn_in_kernel, tm=tm, sub=sub, blocks_per_seq=seq // tm),
        grid=(m // tm, nj),
        in_specs=[
            pl.BlockSpec((tm, d), lambda i, j: (i, 0)),
            pl.BlockSpec((HALO, d), lambda i, j: (jnp.maximum(i * halo_per_block - 1, 0), 0)),
            pl.BlockSpec((1, d), lambda i, j: (0, 0)),
            pl.BlockSpec((d, tn), lambda i, j: (0, j)),
            pl.BlockSpec((d, tn), lambda i, j: (0, nj + j)),
            pl.BlockSpec((CONV_W, tn), lambda i, j: (0, j)),
            pl.BlockSpec((1, tn), lambda i, j: (0, j)),
        ],
        out_specs=pl.BlockSpec((tm, tn), lambda i, j: (i, j)),
        out_shape=jax.ShapeDtypeStruct((m, f), BF16),
        scratch_shapes=[pltpu.VMEM((HALO + tm, d), BF16), pltpu.VMEM((HALO + tm, tn), F32)],
        compiler_params=_params("parallel", "arbitrary"),
        name="ffn_in",
    )(x, x, gain, w_in, w_in, conv_w, conv_b)


def _norm_kernel(x_ref, g_ref, o_ref):
    o_ref[...] = _rms(x_ref[...], g_ref[...])


def rms_norm_rows(x, gain, tm):
    m, d = x.shape
    return pl.pallas_call(
        _norm_kernel,
        grid=(m // tm,),
        in_specs=[pl.BlockSpec((tm, d), lambda i: (i, 0)), pl.BlockSpec((1, d), lambda i: (0, 0))],
        out_specs=pl.BlockSpec((tm, d), lambda i: (i, 0)),
        out_shape=jax.ShapeDtypeStruct((m, d), F32),
        compiler_params=_params("parallel"),
        name="final_norm",
    )(x, gain)


def kernel(x, mem, mem_norm, norm_mix, norm_xattn, norm_ffn, ab_w_in, hgrn_lb_logits, hgrn_norm, gla_w_gk, gla_b_gk, gla_norm, ab_w_out, sb_w_qkv, sb_w_out, xa_w_q, xa_w_kv, xa_w_o, ffn_w_in, ffn_conv_w, ffn_conv_b, ffn_w_out, final_norm):
    b, s, d = x.shape
    depth = norm_mix.shape[0]
    m_len = mem.shape[1]
    rows = b * s
    tm = min(512, s)
    tm_big = min(1024, rows)
    ts = min(512, s)
    tq_sb = min(256, s)

    def row(v):
        return v.reshape(1, -1)

    xr = x.reshape(rows, d)

    w_kv_all = jnp.concatenate([xa_w_kv[l] for l in range(depth)], axis=1).astype(BF16)
    kv = norm_matmul(mem.reshape(b * m_len, d), row(mem_norm), w_kv_all, BF16,
                     tm=min(512, b * m_len), tn=1024)
    kv = kv.reshape(b, m_len, -1)

    ab_main = 4 * HGRN_HEADS * HGRN_HEAD_DIM + GLA_HEADS * (2 * GLA_HEAD_K + 2 * GLA_HEAD_V)

    for layer in range(depth):
        gain = row(norm_mix[layer])
        if layer % 2 == 0:
            a = layer // 2
            w_in = ab_w_in[a]
            w_low = jnp.pad(w_in[:, ab_main:], ((0, 0), (0, LANES - GLA_GATE_RANK))).astype(BF16)
            w_gk = jnp.pad(gla_w_gk[a], ((0, LANES - GLA_GATE_RANK), (0, 0))).astype(BF16)
            p, gk = ab_proj(xr, gain, w_in.astype(BF16), ab_main, w_low, w_gk,
                            row(gla_b_gk[a]), tm=tm_big, tn=1024)
            p = p.reshape(b, s, ab_main)
            o_a = hgrn_mixer(p, hgrn_lb_logits, a, row(hgrn_norm[a]), ts)
            o_b = gla_mixer(p, gk.reshape(b, s, -1), row(gla_norm[a]), ts)
            xr = matmul2_resid(o_a.reshape(rows, -1), o_b.reshape(rows, -1),
                               ab_w_out[a].astype(BF16), xr, tm=tm, tn=d)
        else:
            c = layer // 2
            qkv = norm_matmul(xr, gain, sb_w_qkv[c].astype(BF16), BF16, tm=tm_big, tn=1024,
                              lead_cols=d, lead_scale=(SB_HEAD_DIM ** -0.5) * LOG2_E)
            o = sb_attention(qkv.reshape(b, s, -1), tq_sb, heads=SB_HEADS_PER_STEP)
            xr = matmul_resid(o.reshape(rows, d), sb_w_out[c].astype(BF16), xr, tm=tm, tn=d)

        xr = xattn_layer(xr, row(norm_xattn[layer]), xa_w_q[layer].astype(BF16), kv,
                         xa_w_o[layer].astype(BF16), layer, seq=s, tm=tm)

        act = ffn_in(xr, row(norm_ffn[layer]), ffn_w_in[layer].astype(BF16), ffn_conv_w[layer],
                     row(ffn_conv_b[layer]), seq=s, tm=min(1024, s), tn=512, sub=256)
        xr = matmul_resid(act, ffn_w_out[layer].astype(BF16), xr, tm=tm_big, tn=512)

    return rms_norm_rows(xr, row(final_norm), tm).reshape(b, s, d)
```

```python
import functools

import jax
import jax.numpy as jnp
from jax import lax
from jax.experimental import pallas as pl
from jax.experimental.pallas import tpu as pltpu

F32 = jnp.float32
BF16 = jnp.bfloat16

RMS_EPS = 1e-6
CHUNK = 64
HGRN_HEADS = 8
HGRN_HEAD_DIM = 128
GLA_HEADS = 4
GLA_HEAD_K = 128
GLA_HEAD_V = 256
GLA_GATE_RANK = 16
GLA_GATE_NORMALIZER = 16.0
SB_HEADS = 16
SB_HEAD_DIM = 128
XA_HEADS = 4
CONV_W = 3
LOG2_E = 1.4426950408889634
EXP2_CLAMP = 126.0
SB_HEADS_PER_STEP = 4

LANES = 128
BF16_SUBLANES = 16
VMEM_LIMIT_BYTES = 52 * 1024 * 1024


def _params(*semantics):
    return pltpu.CompilerParams(dimension_semantics=semantics, vmem_limit_bytes=VMEM_LIMIT_BYTES)


def _rms(x, gain):
    return x * lax.rsqrt(jnp.mean(x * x, axis=-1, keepdims=True) + RMS_EPS) * gain


def _sigmoid(x):
    return 1.0 / (1.0 + jnp.exp(-x))


def _softplus(x):
    return jnp.maximum(x, 0.0) + jnp.log(1.0 + jnp.exp(-jnp.abs(x)))


def _dot(a, b):
    return jnp.dot(a, b, preferred_element_type=F32)


def _dot_nt(a, b):
    return lax.dot_general(a, b, (((1,), (1,)), ((), ())), preferred_element_type=F32)


def _dot_tn(a, b):
    return lax.dot_general(a, b, (((0,), (0,)), ((), ())), preferred_element_type=F32)


def _split3(x):
    hi = x.astype(BF16)
    r1 = x - hi.astype(F32)
    mid = r1.astype(BF16)
    lo = (r1 - mid.astype(F32)).astype(BF16)
    return hi, mid, lo


def _norm_matmul_kernel(x_ref, g_ref, w_ref, o_ref, h_ref, *, lead_blocks, lead_scale):
    @pl.when(pl.program_id(1) == 0)
    def _():
        h_ref[...] = _rms(x_ref[...], g_ref[...]).astype(BF16)

    acc = _dot(h_ref[...], w_ref[...])
    if lead_blocks:
        acc = acc * jnp.where(pl.program_id(1) < lead_blocks, lead_scale, 1.0)
    o_ref[...] = acc.astype(o_ref.dtype)


def norm_matmul(x, gain, w, layer, out_dtype, tm, tn, lead_cols=0, lead_scale=1.0):
    m, k = x.shape
    n = w.shape[2]
    assert lead_cols % tn == 0
    return pl.pallas_call(
        functools.partial(_norm_matmul_kernel, lead_blocks=lead_cols // tn, lead_scale=lead_scale),
        grid=(m // tm, n // tn),
        in_specs=[
            pl.BlockSpec((tm, k), lambda i, j: (i, 0)),
            pl.BlockSpec((1, k), lambda i, j: (0, 0)),
            pl.BlockSpec((None, k, tn), lambda i, j: (layer, 0, j)),
        ],
        out_specs=pl.BlockSpec((tm, tn), lambda i, j: (i, j)),
        out_shape=jax.ShapeDtypeStruct((m, n), out_dtype),
        scratch_shapes=[pltpu.VMEM((tm, k), BF16)],
        compiler_params=_params("parallel", "arbitrary"),
        name="norm_matmul",
    )(x, gain, w)


def _ab_proj_kernel(x_ref, g_ref, w_ref, wlow_ref, wgk_ref, bgk_ref, o_ref, gk_ref, h_ref):
    @pl.when(pl.program_id(1) == 0)
    def _():
        h = _rms(x_ref[...], g_ref[...]).astype(BF16)
        h_ref[...] = h
        low = _dot(h, wlow_ref[...])
        pre = _dot(low.astype(BF16), wgk_ref[...]) + bgk_ref[...]
        gk_ref[...] = -_softplus(-pre) * (1.0 / GLA_GATE_NORMALIZER)

    o_ref[...] = _dot(h_ref[...], w_ref[...])


def ab_proj(x, gain, w, layer, n, w_low, w_gk, b_gk, tm, tn):
    m, k = x.shape
    assert n % tn == 0
    ngk = w_gk.shape[1]
    return pl.pallas_call(
        _ab_proj_kernel,
        grid=(m // tm, n // tn),
        in_specs=[
            pl.BlockSpec((tm, k), lambda i, j: (i, 0)),
            pl.BlockSpec((1, k), lambda i, j: (0, 0)),
            pl.BlockSpec((None, k, tn), lambda i, j: (layer, 0, j)),
            pl.BlockSpec((k, LANES), lambda i, j: (0, 0)),
            pl.BlockSpec((LANES, ngk), lambda i, j: (0, 0)),
            pl.BlockSpec((1, ngk), lambda i, j: (0, 0)),
        ],
        out_specs=[
            pl.BlockSpec((tm, tn), lambda i, j: (i, j)),
            pl.BlockSpec((tm, ngk), lambda i, j: (i, 0)),
        ],
        out_shape=[
            jax.ShapeDtypeStruct((m, n), F32),
            jax.ShapeDtypeStruct((m, ngk), F32),
        ],
        scratch_shapes=[pltpu.VMEM((tm, k), BF16)],
        compiler_params=_params("parallel", "arbitrary"),
        name="ab_proj",
    )(x, gain, w, w_low, w_gk, b_gk)


def _matmul_resid_kernel(a_ref, w_ref, r_ref, o_ref):
    o_ref[...] = r_ref[...] + _dot(a_ref[...], w_ref[...])


def matmul_resid(a, w, layer, resid, tm, tn):
    m, k = a.shape
    n = w.shape[2]
    return pl.pallas_call(
        _matmul_resid_kernel,
        grid=(m // tm, n // tn),
        in_specs=[
            pl.BlockSpec((tm, k), lambda i, j: (i, 0)),
            pl.BlockSpec((None, k, tn), lambda i, j: (layer, 0, j)),
            pl.BlockSpec((tm, tn), lambda i, j: (i, j)),
        ],
        out_specs=pl.BlockSpec((tm, tn), lambda i, j: (i, j)),
        out_shape=jax.ShapeDtypeStruct((m, n), F32),
        compiler_params=_params("parallel", "arbitrary"),
        name="matmul_resid",
    )(a, w, resid)


def _matmul2_resid_kernel(a1_ref, a2_ref, w1_ref, w2_ref, r_ref, o_ref):
    o_ref[...] = r_ref[...] + (_dot(a1_ref[...], w1_ref[...]) + _dot(a2_ref[...], w2_ref[...]))


def matmul2_resid(a1, a2, w, layer, resid, tm, tn):
    m, kh = a1.shape
    n = w.shape[2]
    return pl.pallas_call(
        _matmul2_resid_kernel,
        grid=(m // tm, n // tn),
        in_specs=[
            pl.BlockSpec((tm, kh), lambda i, j: (i, 0)),
            pl.BlockSpec((tm, kh), lambda i, j: (i, 0)),
            pl.BlockSpec((None, kh, tn), lambda i, j: (layer, 0, j)),
            pl.BlockSpec((None, kh, tn), lambda i, j: (layer, 1, j)),
            pl.BlockSpec((tm, tn), lambda i, j: (i, j)),
        ],
        out_specs=pl.BlockSpec((tm, tn), lambda i, j: (i, j)),
        out_shape=jax.ShapeDtypeStruct((m, n), F32),
        compiler_params=_params("parallel", "arbitrary"),
        name="matmul2_resid",
    )(a1, a2, w, w, resid)


def _linear_attention_block(q, k, v, g, state_ref):
    n = q.shape[0] // CHUNK
    row = lax.broadcasted_iota(jnp.int32, (CHUNK, CHUNK), 0)
    col = lax.broadcasted_iota(jnp.int32, (CHUNK, CHUNK), 1)
    causal = row >= col
    tri = causal.astype(BF16)
    sl = [slice(c * CHUNK, (c + 1) * CHUNK) for c in range(n)]

    g_terms = [_split3(g[s]) for s in sl]
    b = [_dot(tri, hi) + _dot(tri, mid) + _dot(tri, lo) for hi, mid, lo in g_terms]
    b_mid = [x[CHUNK // 2 - 1:CHUNK // 2] for x in b]
    b_last = [x[CHUNK - 1:] for x in b]
    vb = [v[s].astype(BF16) for s in sl]
    qs = [(q[s] * jnp.exp(x - m)).astype(BF16) for s, x, m in zip(sl, b, b_mid)]
    ks = [(k[s] * jnp.exp(m - x)).astype(BF16) for s, x, m in zip(sl, b, b_mid)]
    kd = [(k[s] * jnp.exp(l - x)).astype(BF16) for s, x, l in zip(sl, b, b_last)]
    qd = [(q[s] * jnp.exp(x)).astype(BF16) for s, x in zip(sl, b)]

    scores = [jnp.where(causal, _dot_nt(a, c), 0.0).astype(BF16) for a, c in zip(qs, ks)]
    update = [_dot_tn(a, c) for a, c in zip(vb, kd)]
    o_intra = [_dot(a, c) for a, c in zip(scores, vb)]

    state = state_ref[...]
    states = []
    for c in range(n):
        states.append(state.astype(BF16))
        state = state * jnp.exp(b_last[c]) + update[c]
    state_ref[...] = state

    o = [oi + _dot_nt(a, st) for oi, a, st in zip(o_intra, qd, states)]
    return jnp.concatenate(o, axis=0)


def _hgrn_kernel(aq_ref, af_ref, ai_ref, ag_ref, lbl_ref, gain_ref, o_ref, state_ref, *, lb_rows):
    @pl.when(pl.program_id(2) == 0)
    def _():
        state_ref[...] = jnp.zeros_like(state_ref)

    logits = lbl_ref[...]
    e = jnp.exp(logits - jnp.max(logits, axis=0, keepdims=True))
    lb = jnp.sum(e[:lb_rows], axis=0, keepdims=True) / jnp.sum(e, axis=0, keepdims=True)

    a_q = aq_ref[0]
    f = lb + (1.0 - lb) * _sigmoid(af_ref[0])
    o = _linear_attention_block(a_q * _sigmoid(a_q), 1.0 - f, ai_ref[0], jnp.log(f), state_ref)
    o_ref[0] = (_rms(o, gain_ref[...]) * _sigmoid(ag_ref[0])).astype(o_ref.dtype)


def hgrn_mixer(p, lb_logits, a_idx, gain, ts):
    b, s, _ = p.shape
    nh, hd = HGRN_HEADS, HGRN_HEAD_DIM
    nlb = lb_logits.shape[0]

    def col(part):
        return pl.BlockSpec((1, ts, hd), lambda bi, h, si: (bi, si, part * nh + h))

    return pl.pallas_call(
        functools.partial(_hgrn_kernel, lb_rows=a_idx + 1),
        grid=(b, nh, s // ts),
        in_specs=[
            col(0), col(1), col(2), col(3),
            pl.BlockSpec((nlb, hd), lambda bi, h, si: (0, h)),
            pl.BlockSpec((1, hd), lambda bi, h, si: (0, 0)),
        ],
        out_specs=pl.BlockSpec((1, ts, hd), lambda bi, h, si: (bi, si, h)),
        out_shape=jax.ShapeDtypeStruct((b, s, nh * hd), BF16),
        scratch_shapes=[pltpu.VMEM((hd, hd), F32)],
        compiler_params=_params("parallel", "parallel", "arbitrary"),
        name="hgrn_mixer",
    )(p, p, p, p, lb_logits, gain)


def _gla_kernel(q_ref, k_ref, v_ref, gg_ref, gk_ref, gain_ref, o_ref, state_ref):
    @pl.when(pl.program_id(2) == 0)
    def _():
        state_ref[...] = jnp.zeros_like(state_ref)

    q = q_ref[0] * (GLA_HEAD_K ** -0.5)
    o = _linear_attention_block(q, k_ref[0], v_ref[0], gk_ref[0], state_ref)
    g_g = gg_ref[0]
    o_ref[0] = (_rms(o, gain_ref[...]) * (g_g * _sigmoid(g_g))).astype(o_ref.dtype)


def gla_mixer(p, gk, gain, ts):
    b, s, _ = p.shape
    nh, kd, vd = GLA_HEADS, GLA_HEAD_K, GLA_HEAD_V
    base = 4 * HGRN_HEADS * HGRN_HEAD_DIM
    q0 = base // kd
    k0 = q0 + nh
    v0 = (base + 2 * nh * kd) // vd
    g0 = v0 + nh
    return pl.pallas_call(
        _gla_kernel,
        grid=(b, nh, s // ts),
        in_specs=[
            pl.BlockSpec((1, ts, kd), lambda bi, h, si: (bi, si, q0 + h)),
            pl.BlockSpec((1, ts, kd), lambda bi, h, si: (bi, si, k0 + h)),
            pl.BlockSpec((1, ts, vd), lambda bi, h, si: (bi, si, v0 + h)),
            pl.BlockSpec((1, ts, vd), lambda bi, h, si: (bi, si, g0 + h)),
            pl.BlockSpec((1, ts, kd), lambda bi, h, si: (bi, si, h)),
            pl.BlockSpec((1, vd), lambda bi, h, si: (0, 0)),
        ],
        out_specs=pl.BlockSpec((1, ts, vd), lambda bi, h, si: (bi, si, h)),
        out_shape=jax.ShapeDtypeStruct((b, s, nh * vd), BF16),
        scratch_shapes=[pltpu.VMEM((vd, kd), F32)],
        compiler_params=_params("parallel", "parallel", "arbitrary"),
        name="gla_mixer",
    )(p, p, p, p, gk, gain)


def _sb_kernel(q_ref, k_ref, v_ref, o_ref, *, tq, heads):
    qi = pl.program_id(2)
    hd = SB_HEAD_DIM
    row = lax.broadcasted_iota(jnp.int32, (tq, tq), 0)
    col = lax.broadcasted_iota(jnp.int32, (tq, tq), 1)
    behind_ones = (row > col).astype(BF16)
    strictly_causal = col < row
    head_cols = [slice(h * hd, (h + 1) * hd) for h in range(heads)]

    def key_block(j, carry, diagonal):
        start = pl.multiple_of(j * tq, tq)
        acc = [c[0] for c in carry]
        later = [c[1] for c in carry]
        z2 = [_dot_nt(q_ref[0, :, hc], k_ref[0, pl.ds(start, tq), hc]) for hc in head_cols]
        sp2 = [jnp.maximum(z, jnp.log(1.0 + jnp.exp2(jnp.minimum(z, EXP2_CLAMP))) * LOG2_E)
               for z in z2]
        if diagonal:
            sp2 = [jnp.where(strictly_causal, x, 0.0) for x in sp2]
        own = [z - x - lt for z, x, lt in zip(z2, sp2, later)]
        behind = [_dot(x.astype(BF16), behind_ones) for x in sp2]
        log2_a = [o - bh for o, bh in zip(own, behind)]
        if diagonal:
            log2_a = [jnp.where(strictly_causal, x, -jnp.inf) for x in log2_a]
        att = [jnp.exp2(x).astype(BF16) for x in log2_a]
        acc = [a + _dot(p, v_ref[0, pl.ds(start, tq), hc]) for a, p, hc in zip(acc, att, head_cols)]
        later = [lt + (x[:, 0:1] + bh[:, 0:1]) for lt, x, bh in zip(later, sp2, behind)]
        return tuple(zip(acc, later))

    zero = (jnp.zeros((tq, hd), F32), jnp.zeros((tq, 1), F32))
    carry = key_block(qi, (zero,) * heads, True)
    carry = lax.fori_loop(0, qi, lambda t, c: key_block(qi - 1 - t, c, False), carry)
    for h, hc in enumerate(head_cols):
        o_ref[0, :, hc] = carry[h][0].astype(o_ref.dtype)


def sb_attention(qkv, tq, heads):
    b, s, _ = qkv.shape
    ng = SB_HEADS // heads
    w = heads * SB_HEAD_DIM
    return pl.pallas_call(
        functools.partial(_sb_kernel, tq=tq, heads=heads),
        grid=(b, ng, s // tq),
        in_specs=[
            pl.BlockSpec((1, tq, w), lambda bi, g, qi: (bi, qi, g)),
            pl.BlockSpec((1, s, w), lambda bi, g, qi: (bi, 0, ng + g)),
            pl.BlockSpec((1, s, w), lambda bi, g, qi: (bi, 0, 2 * ng + g)),
        ],
        out_specs=pl.BlockSpec((1, tq, w), lambda bi, g, qi: (bi, qi, g)),
        out_shape=jax.ShapeDtypeStruct((b, s, SB_HEADS * SB_HEAD_DIM), BF16),
        compiler_params=_params("parallel", "parallel", "arbitrary"),
        name="sb_attention",
    )(qkv, qkv, qkv)


def _xattn_layer_kernel(x_ref, g_ref, wq_ref, k_ref, v_ref, wo_ref, o_ref):
    x = x_ref[...]
    d = x.shape[-1]
    hd = d // XA_HEADS
    head_cols = [slice(h * hd, (h + 1) * hd) for h in range(XA_HEADS)]
    h_in = _rms(x, g_ref[...]).astype(BF16)
    q = [_dot(h_in, wq_ref[:, hc]).astype(BF16) for hc in head_cols]
    scores = [_dot_nt(qh, k_ref[0, :, hc]) * (hd ** -0.5) for qh, hc in zip(q, head_cols)]
    e = [jnp.exp(sc - jnp.max(sc, axis=-1, keepdims=True)) for sc in scores]
    probs = [(eh / jnp.sum(eh, axis=-1, keepdims=True)).astype(BF16) for eh in e]
    o = [_dot(ph, v_ref[0, :, hc]).astype(BF16) for ph, hc in zip(probs, head_cols)]
    o_ref[...] = x + _dot(jnp.concatenate(o, axis=1), wo_ref[...])


def xattn_layer(x, gain, w_q, kv, w_o, layer, seq, tm):
    rows, d = x.shape
    m = kv.shape[1]
    blocks_per_seq = seq // tm
    whole = functools.partial(pl.BlockSpec, (None, d, d), lambda i: (layer, 0, 0),
                              pipeline_mode=pl.Buffered(1))
    return pl.pallas_call(
        _xattn_layer_kernel,
        grid=(rows // tm,),
        in_specs=[
            pl.BlockSpec((tm, d), lambda i: (i, 0)),
            pl.BlockSpec((1, d), lambda i: (0, 0)),
            whole(),
            pl.BlockSpec((1, m, d), lambda i: (i // blocks_per_seq, 0, 0)),
            pl.BlockSpec((1, m, d), lambda i: (i // blocks_per_seq, 0, 1)),
            whole(),
        ],
        out_specs=pl.BlockSpec((tm, d), lambda i: (i, 0)),
        out_shape=jax.ShapeDtypeStruct((rows, d), F32),
        compiler_params=_params("parallel"),
        name="xattn_layer",
    )(x, gain, w_q, kv, kv, w_o)


HALO = BF16_SUBLANES


def _ffn_in_kernel(x_ref, halo_ref, g_ref, wu_ref, wg_ref, cw_ref, cb_ref, o_ref, h_ref, gs_ref,
                   *, tm, sub, blocks_per_seq):
    inside = pl.program_id(0) % blocks_per_seq != 0

    @pl.when(pl.program_id(1) == 0)
    def _():
        h_ref[HALO:, :] = _rms(x_ref[...], g_ref[...]).astype(BF16)
        hh = _rms(halo_ref[...], g_ref[...])
        h_ref[:HALO, :] = jnp.where(inside, hh, 0.0).astype(BF16)

    cw = cw_ref[...]
    cb = cb_ref[...]
    for r in range(tm // sub):
        lo = HALO + r * sub
        first = 0 if r == 0 else lo
        gs_ref[first:lo + sub, :] = _dot(h_ref[first:lo + sub, :], wg_ref[...])
        u = _dot(h_ref[lo:lo + sub, :], wu_ref[...])
        conv = 0.0
        for tap in range(CONV_W):
            t0 = lo - (CONV_W - 1) + tap
            conv = conv + cw[tap:tap + 1] * gs_ref[t0:t0 + sub, :]
        gc = cb + conv
        o_ref[r * sub:(r + 1) * sub, :] = (gc * _sigmoid(gc) * u).astype(o_ref.dtype)


def ffn_in(x, gain, w_in, layer, conv_w, conv_b, seq, tm, tn, sub):
    m, d = x.shape
    f = w_in.shape[2] // 2
    nj = f // tn
    halo_per_block = tm // HALO
    return pl.pallas_call(
        functools.partial(_ffn_in_kernel, tm=tm, sub=sub, blocks_per_seq=seq // tm),
        grid=(m // tm, nj),
        in_specs=[
            pl.BlockSpec((tm, d), lambda i, j: (i, 0)),
            pl.BlockSpec((HALO, d), lambda i, j: (jnp.maximum(i * halo_per_block - 1, 0), 0)),
            pl.BlockSpec((1, d), lambda i, j: (0, 0)),
            pl.BlockSpec((None, d, tn), lambda i, j: (layer, 0, j)),
            pl.BlockSpec((None, d, tn), lambda i, j: (layer, 0, nj + j)),
            pl.BlockSpec((CONV_W, tn), lambda i, j: (0, j)),
            pl.BlockSpec((1, tn), lambda i, j: (0, j)),
        ],
        out_specs=pl.BlockSpec((tm, tn), lambda i, j: (i, j)),
        out_shape=jax.ShapeDtypeStruct((m, f), BF16),
        scratch_shapes=[pltpu.VMEM((HALO + tm, d), BF16), pltpu.VMEM((HALO + tm, tn), F32)],
        compiler_params=_params("parallel", "arbitrary"),
        name="ffn_in",
    )(x, x, gain, w_in, w_in, conv_w, conv_b)


def _norm_kernel(x_ref, g_ref, o_ref):
    o_ref[...] = _rms(x_ref[...], g_ref[...])


def rms_norm_rows(x, gain, tm):
    m, d = x.shape
    return pl.pallas_call(
        _norm_kernel,
        grid=(m // tm,),
        in_specs=[pl.BlockSpec((tm, d), lambda i: (i, 0)), pl.BlockSpec((1, d), lambda i: (0, 0))],
        out_specs=pl.BlockSpec((tm, d), lambda i: (i, 0)),
        out_shape=jax.ShapeDtypeStruct((m, d), F32),
        compiler_params=_params("parallel"),
        name="final_norm",
    )(x, gain)


def kernel(x, mem, mem_norm, norm_mix, norm_xattn, norm_ffn, ab_w_in, hgrn_lb_logits, hgrn_norm, gla_w_gk, gla_b_gk, gla_norm, ab_w_out, sb_w_qkv, sb_w_out, xa_w_q, xa_w_kv, xa_w_o, ffn_w_in, ffn_conv_w, ffn_conv_b, ffn_w_out, final_norm):
    b, s, d = x.shape
    depth = norm_mix.shape[0]
    m_len = mem.shape[1]
    rows = b * s
    tm = min(512, s)
    tm_big = min(1024, rows)
    ts = min(1024, s)
    tq_sb = min(256, s)

    def row(v):
        return v.reshape(1, -1)

    xr = x.reshape(rows, d)

    ab_w_in_b, ab_w_out_b = ab_w_in.astype(BF16), ab_w_out.astype(BF16)
    sb_w_qkv_b, sb_w_out_b = sb_w_qkv.astype(BF16), sb_w_out.astype(BF16)
    xa_w_q_b, xa_w_kv_b, xa_w_o_b = xa_w_q.astype(BF16), xa_w_kv.astype(BF16), xa_w_o.astype(BF16)
    ffn_w_in_b, ffn_w_out_b = ffn_w_in.astype(BF16), ffn_w_out.astype(BF16)

    ab_main = 4 * HGRN_HEADS * HGRN_HEAD_DIM + GLA_HEADS * (2 * GLA_HEAD_K + 2 * GLA_HEAD_V)
    mem_rows = mem.reshape(b * m_len, d)

    for layer in range(depth):
        gain = row(norm_mix[layer])
        if layer % 2 == 0:
            a = layer // 2
            w_low = jnp.pad(ab_w_in[a][:, ab_main:], ((0, 0), (0, LANES - GLA_GATE_RANK))).astype(BF16)
            w_gk = jnp.pad(gla_w_gk[a], ((0, LANES - GLA_GATE_RANK), (0, 0))).astype(BF16)
            p, gk = ab_proj(xr, gain, ab_w_in_b, a, ab_main, w_low, w_gk,
                            row(gla_b_gk[a]), tm=tm_big, tn=1024)
            p = p.reshape(b, s, ab_main)
            o_a = hgrn_mixer(p, hgrn_lb_logits, a, row(hgrn_norm[a]), ts)
            o_b = gla_mixer(p, gk.reshape(b, s, -1), row(gla_norm[a]), ts)
            xr = matmul2_resid(o_a.reshape(rows, -1), o_b.reshape(rows, -1),
                               ab_w_out_b, a, xr, tm=tm, tn=d)
        else:
            c = layer // 2
            qkv = norm_matmul(xr, gain, sb_w_qkv_b, c, BF16, tm=tm_big, tn=1024,
                              lead_cols=d, lead_scale=(SB_HEAD_DIM ** -0.5) * LOG2_E)
            o = sb_attention(qkv.reshape(b, s, -1), tq_sb, heads=SB_HEADS_PER_STEP)
            xr = matmul_resid(o.reshape(rows, d), sb_w_out_b, c, xr, tm=tm, tn=d)

        kv = norm_matmul(mem_rows, row(mem_norm), xa_w_kv_b, layer, BF16,
                         tm=min(1024, b * m_len), tn=d)
        xr = xattn_layer(xr, row(norm_xattn[layer]), xa_w_q_b, kv.reshape(b, m_len, 2 * d),
                         xa_w_o_b, layer, seq=s, tm=tm)

        act = ffn_in(xr, row(norm_ffn[layer]), ffn_w_in_b, layer, ffn_conv_w[layer],
                     row(ffn_conv_b[layer]), seq=s, tm=min(1024, s), tn=512, sub=256)
        xr = matmul_resid(act, ffn_w_out_b, layer, xr, tm=tm_big, tn=512)

    return rms_norm_rows(xr, row(final_norm), tm).reshape(b, s, d)
```

```python
import functools

import jax
import jax.numpy as jnp
from jax import lax
from jax.experimental import pallas as pl
from jax.experimental.pallas import tpu as pltpu

F32 = jnp.float32
BF16 = jnp.bfloat16

RMS_EPS = 1e-6
CHUNK = 64
HGRN_HEADS = 8
HGRN_HEAD_DIM = 128
GLA_HEADS = 4
GLA_HEAD_K = 128
GLA_HEAD_V = 256
GLA_GATE_RANK = 16
GLA_GATE_NORMALIZER = 16.0
SB_HEADS = 16
SB_HEAD_DIM = 128
XA_HEADS = 4
CONV_W = 3
LOG2_E = 1.4426950408889634
EXP2_CLAMP = 126.0
SB_HEADS_PER_STEP = 4

LANES = 128
BF16_SUBLANES = 16
VMEM_LIMIT_BYTES = 52 * 1024 * 1024
NORM_SUB_ROWS = 256


def _params(*semantics):
    return pltpu.CompilerParams(dimension_semantics=semantics, vmem_limit_bytes=VMEM_LIMIT_BYTES)


def _rms(x, gain):
    return x * lax.rsqrt(jnp.mean(x * x, axis=-1, keepdims=True) + RMS_EPS) * gain


def _sigmoid(x):
    return 1.0 / (1.0 + jnp.exp(-x))


def _softplus(x):
    return jnp.maximum(x, 0.0) + jnp.log(1.0 + jnp.exp(-jnp.abs(x)))


def _dot(a, b):
    return jnp.dot(a, b, preferred_element_type=F32)


def _dot_nt(a, b):
    return lax.dot_general(a, b, (((1,), (1,)), ((), ())), preferred_element_type=F32)


def _dot_tn(a, b):
    return lax.dot_general(a, b, (((0,), (0,)), ((), ())), preferred_element_type=F32)


def _split3(x):
    hi = x.astype(BF16)
    r1 = x - hi.astype(F32)
    mid = r1.astype(BF16)
    lo = (r1 - mid.astype(F32)).astype(BF16)
    return hi, mid, lo


def _row_blocks(total, size):
    return [slice(r, r + size) for r in range(0, total, size)]


def _norm_matmul_kernel(x_ref, g_ref, w_ref, o_ref, h_ref, *, lead_blocks, lead_scale, sub):
    j = pl.program_id(1)

    def project(h):
        acc = _dot(h, w_ref[...])
        if lead_blocks:
            acc = acc * jnp.where(j < lead_blocks, lead_scale, 1.0)
        return acc.astype(o_ref.dtype)

    @pl.when(j == 0)
    def _():
        for rows in _row_blocks(x_ref.shape[0], sub):
            h = _rms(x_ref[rows, :], g_ref[...]).astype(BF16)
            h_ref[rows, :] = h
            o_ref[rows, :] = project(h)

    @pl.when(j != 0)
    def _():
        o_ref[...] = project(h_ref[...])


def norm_matmul(x, gain, w, layer, out_dtype, tm, tn, lead_cols=0, lead_scale=1.0):
    m, k = x.shape
    n = w.shape[2]
    assert lead_cols % tn == 0
    return pl.pallas_call(
        functools.partial(_norm_matmul_kernel, lead_blocks=lead_cols // tn, lead_scale=lead_scale,
                          sub=min(NORM_SUB_ROWS, tm)),
        grid=(m // tm, n // tn),
        in_specs=[
            pl.BlockSpec((tm, k), lambda i, j: (i, 0)),
            pl.BlockSpec((1, k), lambda i, j: (0, 0)),
            pl.BlockSpec((None, k, tn), lambda i, j: (layer, 0, j)),
        ],
        out_specs=pl.BlockSpec((tm, tn), lambda i, j: (i, j)),
        out_shape=jax.ShapeDtypeStruct((m, n), out_dtype),
        scratch_shapes=[pltpu.VMEM((tm, k), BF16)],
        compiler_params=_params("parallel", "arbitrary"),
        name="norm_matmul",
    )(x, gain, w)


def _ab_proj_kernel(x_ref, g_ref, w_ref, wlow_ref, wgk_ref, bgk_ref, o_ref, gk_ref, h_ref, *, sub):
    @pl.when(pl.program_id(1) == 0)
    def _():
        for rows in _row_blocks(x_ref.shape[0], sub):
            h = _rms(x_ref[rows, :], g_ref[...]).astype(BF16)
            h_ref[rows, :] = h
            o_ref[rows, :] = _dot(h, w_ref[...])
            low = _dot(h, wlow_ref[...])
            pre = _dot(low.astype(BF16), wgk_ref[...]) + bgk_ref[...]
            gk_ref[rows, :] = -_softplus(-pre) * (1.0 / GLA_GATE_NORMALIZER)

    @pl.when(pl.program_id(1) != 0)
    def _():
        o_ref[...] = _dot(h_ref[...], w_ref[...])


def ab_proj(x, gain, w, layer, n, w_low, w_gk, b_gk, tm, tn):
    m, k = x.shape
    assert n % tn == 0
    ngk = w_gk.shape[1]
    return pl.pallas_call(
        functools.partial(_ab_proj_kernel, sub=min(NORM_SUB_ROWS, tm)),
        grid=(m // tm, n // tn),
        in_specs=[
            pl.BlockSpec((tm, k), lambda i, j: (i, 0)),
            pl.BlockSpec((1, k), lambda i, j: (0, 0)),
            pl.BlockSpec((None, k, tn), lambda i, j: (layer, 0, j)),
            pl.BlockSpec((k, LANES), lambda i, j: (0, 0)),
            pl.BlockSpec((LANES, ngk), lambda i, j: (0, 0)),
            pl.BlockSpec((1, ngk), lambda i, j: (0, 0)),
        ],
        out_specs=[
            pl.BlockSpec((tm, tn), lambda i, j: (i, j)),
            pl.BlockSpec((tm, ngk), lambda i, j: (i, 0)),
        ],
        out_shape=[
            jax.ShapeDtypeStruct((m, n), F32),
            jax.ShapeDtypeStruct((m, ngk), F32),
        ],
        scratch_shapes=[pltpu.VMEM((tm, k), BF16)],
        compiler_params=_params("parallel", "arbitrary"),
        name="ab_proj",
    )(x, gain, w, w_low, w_gk, b_gk)


def _matmul_resid_kernel(a_ref, w_ref, r_ref, o_ref):
    o_ref[...] = r_ref[...] + _dot(a_ref[...], w_ref[...])


def matmul_resid(a, w, layer, resid, tm, tn):
    m, k = a.shape
    n = w.shape[2]
    return pl.pallas_call(
        _matmul_resid_kernel,
        grid=(m // tm, n // tn),
        in_specs=[
            pl.BlockSpec((tm, k), lambda i, j: (i, 0)),
            pl.BlockSpec((None, k, tn), lambda i, j: (layer, 0, j)),
            pl.BlockSpec((tm, tn), lambda i, j: (i, j)),
        ],
        out_specs=pl.BlockSpec((tm, tn), lambda i, j: (i, j)),
        out_shape=jax.ShapeDtypeStruct((m, n), F32),
        compiler_params=_params("parallel", "arbitrary"),
        name="matmul_resid",
    )(a, w, resid)


def _matmul2_resid_kernel(a1_ref, a2_ref, w1_ref, w2_ref, r_ref, o_ref):
    o_ref[...] = r_ref[...] + (_dot(a1_ref[...], w1_ref[...]) + _dot(a2_ref[...], w2_ref[...]))


def matmul2_resid(a1, a2, w, layer, resid, tm, tn):
    m, kh = a1.shape
    n = w.shape[2]
    return pl.pallas_call(
        _matmul2_resid_kernel,
        grid=(m // tm, n // tn),
        in_specs=[
            pl.BlockSpec((tm, kh), lambda i, j: (i, 0)),
            pl.BlockSpec((tm, kh), lambda i, j: (i, 0)),
            pl.BlockSpec((None, kh, tn), lambda i, j: (layer, 0, j)),
            pl.BlockSpec((None, kh, tn), lambda i, j: (layer, 1, j)),
            pl.BlockSpec((tm, tn), lambda i, j: (i, j)),
        ],
        out_specs=pl.BlockSpec((tm, tn), lambda i, j: (i, j)),
        out_shape=jax.ShapeDtypeStruct((m, n), F32),
        compiler_params=_params("parallel", "arbitrary"),
        name="matmul2_resid",
    )(a1, a2, w, w, resid)


def _linear_attention_block(q, k, v, g, state_ref):
    n = q.shape[0] // CHUNK
    row = lax.broadcasted_iota(jnp.int32, (CHUNK, CHUNK), 0)
    col = lax.broadcasted_iota(jnp.int32, (CHUNK, CHUNK), 1)
    causal = row >= col
    tri = causal.astype(BF16)
    sl = [slice(c * CHUNK, (c + 1) * CHUNK) for c in range(n)]

    g_terms = [_split3(g[s]) for s in sl]
    b = [_dot(tri, hi) + _dot(tri, mid) + _dot(tri, lo) for hi, mid, lo in g_terms]
    b_mid = [x[CHUNK // 2 - 1:CHUNK // 2] for x in b]
    b_last = [x[CHUNK - 1:] for x in b]
    vb = [v[s].astype(BF16) for s in sl]
    qs = [(q[s] * jnp.exp(x - m)).astype(BF16) for s, x, m in zip(sl, b, b_mid)]
    ks = [(k[s] * jnp.exp(m - x)).astype(BF16) for s, x, m in zip(sl, b, b_mid)]
    kd = [(k[s] * jnp.exp(l - x)).astype(BF16) for s, x, l in zip(sl, b, b_last)]
    qd = [(q[s] * jnp.exp(x)).astype(BF16) for s, x in zip(sl, b)]

    scores = [jnp.where(causal, _dot_nt(a, c), 0.0).astype(BF16) for a, c in zip(qs, ks)]
    update = [_dot_tn(a, c) for a, c in zip(vb, kd)]
    o_intra = [_dot(a, c) for a, c in zip(scores, vb)]

    state = state_ref[...]
    states = []
    for c in range(n):
        states.append(state.astype(BF16))
        state = state * jnp.exp(b_last[c]) + update[c]
    state_ref[...] = state

    o = [oi + _dot_nt(a, st) for oi, a, st in zip(o_intra, qd, states)]
    return jnp.concatenate(o, axis=0)


def _hgrn_kernel(aq_ref, af_ref, ai_ref, ag_ref, lbl_ref, gain_ref, o_ref, state_ref, *, lb_rows):
    @pl.when(pl.program_id(2) == 0)
    def _():
        state_ref[...] = jnp.zeros_like(state_ref)

    logits = lbl_ref[...]
    e = jnp.exp(logits - jnp.max(logits, axis=0, keepdims=True))
    lb = jnp.sum(e[:lb_rows], axis=0, keepdims=True) / jnp.sum(e, axis=0, keepdims=True)

    a_q = aq_ref[0]
    f = lb + (1.0 - lb) * _sigmoid(af_ref[0])
    o = _linear_attention_block(a_q * _sigmoid(a_q), 1.0 - f, ai_ref[0], jnp.log(f), state_ref)
    o_ref[0] = (_rms(o, gain_ref[...]) * _sigmoid(ag_ref[0])).astype(o_ref.dtype)


def hgrn_mixer(p, lb_logits, a_idx, gain, ts):
    b, s, _ = p.shape
    nh, hd = HGRN_HEADS, HGRN_HEAD_DIM
    nlb = lb_logits.shape[0]

    def col(part):
        return pl.BlockSpec((1, ts, hd), lambda bi, h, si: (bi, si, part * nh + h))

    return pl.pallas_call(
        functools.partial(_hgrn_kernel, lb_rows=a_idx + 1),
        grid=(b, nh, s // ts),
        in_specs=[
            col(0), col(1), col(2), col(3),
            pl.BlockSpec((nlb, hd), lambda bi, h, si: (0, h)),
            pl.BlockSpec((1, hd), lambda bi, h, si: (0, 0)),
        ],
        out_specs=pl.BlockSpec((1, ts, hd), lambda bi, h, si: (bi, si, h)),
        out_shape=jax.ShapeDtypeStruct((b, s, nh * hd), BF16),
        scratch_shapes=[pltpu.VMEM((hd, hd), F32)],
        compiler_params=_params("parallel", "parallel", "arbitrary"),
        name="hgrn_mixer",
    )(p, p, p, p, lb_logits, gain)


def _gla_kernel(q_ref, k_ref, v_ref, gg_ref, gk_ref, gain_ref, o_ref, state_ref):
    @pl.when(pl.program_id(2) == 0)
    def _():
        state_ref[...] = jnp.zeros_like(state_ref)

    q = q_ref[0] * (GLA_HEAD_K ** -0.5)
    o = _linear_attention_block(q, k_ref[0], v_ref[0], gk_ref[0], state_ref)
    g_g = gg_ref[0]
    o_ref[0] = (_rms(o, gain_ref[...]) * (g_g * _sigmoid(g_g))).astype(o_ref.dtype)


def gla_mixer(p, gk, gain, ts):
    b, s, _ = p.shape
    nh, kd, vd = GLA_HEADS, GLA_HEAD_K, GLA_HEAD_V
    base = 4 * HGRN_HEADS * HGRN_HEAD_DIM
    q0 = base // kd
    k0 = q0 + nh
    v0 = (base + 2 * nh * kd) // vd
    g0 = v0 + nh
    return pl.pallas_call(
        _gla_kernel,
        grid=(b, nh, s // ts),
        in_specs=[
            pl.BlockSpec((1, ts, kd), lambda bi, h, si: (bi, si, q0 + h)),
            pl.BlockSpec((1, ts, kd), lambda bi, h, si: (bi, si, k0 + h)),
            pl.BlockSpec((1, ts, vd), lambda bi, h, si: (bi, si, v0 + h)),
            pl.BlockSpec((1, ts, vd), lambda bi, h, si: (bi, si, g0 + h)),
            pl.BlockSpec((1, ts, kd), lambda bi, h, si: (bi, si, h)),
            pl.BlockSpec((1, vd), lambda bi, h, si: (0, 0)),
        ],
        out_specs=pl.BlockSpec((1, ts, vd), lambda bi, h, si: (bi, si, h)),
        out_shape=jax.ShapeDtypeStruct((b, s, nh * vd), BF16),
        scratch_shapes=[pltpu.VMEM((vd, kd), F32)],
        compiler_params=_params("parallel", "parallel", "arbitrary"),
        name="gla_mixer",
    )(p, p, p, p, gk, gain)


def _sb_kernel(q_ref, k_ref, v_ref, o_ref, *, tq, heads):
    qi = pl.program_id(2)
    hd = SB_HEAD_DIM
    row = lax.broadcasted_iota(jnp.int32, (tq, tq), 0)
    col = lax.broadcasted_iota(jnp.int32, (tq, tq), 1)
    behind_ones = (row > col).astype(BF16)
    strictly_causal = col < row
    head_cols = [slice(h * hd, (h + 1) * hd) for h in range(heads)]

    def key_block(j, carry, diagonal):
        start = pl.multiple_of(j * tq, tq)
        acc = [c[0] for c in carry]
        later = [c[1] for c in carry]
        z2 = [_dot_nt(q_ref[0, :, hc], k_ref[0, pl.ds(start, tq), hc]) for hc in head_cols]
        sp2 = [jnp.maximum(z, jnp.log(1.0 + jnp.exp2(jnp.minimum(z, EXP2_CLAMP))) * LOG2_E)
               for z in z2]
        if diagonal:
            sp2 = [jnp.where(strictly_causal, x, 0.0) for x in sp2]
        own = [z - x - lt for z, x, lt in zip(z2, sp2, later)]
        behind = [_dot(x.astype(BF16), behind_ones) for x in sp2]
        log2_a = [o - bh for o, bh in zip(own, behind)]
        if diagonal:
            log2_a = [jnp.where(strictly_causal, x, -jnp.inf) for x in log2_a]
        att = [jnp.exp2(x).astype(BF16) for x in log2_a]
        acc = [a + _dot(p, v_ref[0, pl.ds(start, tq), hc]) for a, p, hc in zip(acc, att, head_cols)]
        later = [lt + (x[:, 0:1] + bh[:, 0:1]) for lt, x, bh in zip(later, sp2, behind)]
        return tuple(zip(acc, later))

    zero = (jnp.zeros((tq, hd), F32), jnp.zeros((tq, 1), F32))
    carry = key_block(qi, (zero,) * heads, True)
    carry = lax.fori_loop(0, qi, lambda t, c: key_block(qi - 1 - t, c, False), carry)
    for h, hc in enumerate(head_cols):
        o_ref[0, :, hc] = carry[h][0].astype(o_ref.dtype)


def sb_attention(qkv, tq, heads):
    b, s, _ = qkv.shape
    ng = SB_HEADS // heads
    w = heads * SB_HEAD_DIM
    return pl.pallas_call(
        functools.partial(_sb_kernel, tq=tq, heads=heads),
        grid=(b, ng, s // tq),
        in_specs=[
            pl.BlockSpec((1, tq, w), lambda bi, g, qi: (bi, qi, g)),
            pl.BlockSpec((1, s, w), lambda bi, g, qi: (bi, 0, ng + g)),
            pl.BlockSpec((1, s, w), lambda bi, g, qi: (bi, 0, 2 * ng + g)),
        ],
        out_specs=pl.BlockSpec((1, tq, w), lambda bi, g, qi: (bi, qi, g)),
        out_shape=jax.ShapeDtypeStruct((b, s, SB_HEADS * SB_HEAD_DIM), BF16),
        compiler_params=_params("parallel", "parallel", "arbitrary"),
        name="sb_attention",
    )(qkv, qkv, qkv)


def _xattn_layer_kernel(x_ref, g_ref, wq_ref, k_ref, v_ref, wo_ref, o_ref):
    x = x_ref[...]
    d = x.shape[-1]
    hd = d // XA_HEADS
    head_cols = [slice(h * hd, (h + 1) * hd) for h in range(XA_HEADS)]
    h_in = [_rms(x[rows], g_ref[...]).astype(BF16)
            for rows in _row_blocks(x.shape[0], min(NORM_SUB_ROWS, x.shape[0]))]
    q = [jnp.concatenate([_dot(hp, wq_ref[:, hc]).astype(BF16) for hp in h_in], axis=0)
         for hc in head_cols]
    scores = [_dot_nt(qh, k_ref[0, :, hc]) * (hd ** -0.5) for qh, hc in zip(q, head_cols)]
    e = [jnp.exp(sc - jnp.max(sc, axis=-1, keepdims=True)) for sc in scores]
    probs = [(eh / jnp.sum(eh, axis=-1, keepdims=True)).astype(BF16) for eh in e]
    o = [_dot(ph, v_ref[0, :, hc]).astype(BF16) for ph, hc in zip(probs, head_cols)]
    o_ref[...] = x + _dot(jnp.concatenate(o, axis=1), wo_ref[...])


def xattn_layer(x, gain, w_q, kv, w_o, layer, seq, tm):
    rows, d = x.shape
    m = kv.shape[1]
    blocks_per_seq = seq // tm
    whole = functools.partial(pl.BlockSpec, (None, d, d), lambda i: (layer, 0, 0),
                              pipeline_mode=pl.Buffered(1))
    return pl.pallas_call(
        _xattn_layer_kernel,
        grid=(rows // tm,),
        in_specs=[
            pl.BlockSpec((tm, d), lambda i: (i, 0)),
            pl.BlockSpec((1, d), lambda i: (0, 0)),
            whole(),
            pl.BlockSpec((1, m, d), lambda i: (i // blocks_per_seq, 0, 0)),
            pl.BlockSpec((1, m, d), lambda i: (i // blocks_per_seq, 0, 1)),
            whole(),
        ],
        out_specs=pl.BlockSpec((tm, d), lambda i: (i, 0)),
        out_shape=jax.ShapeDtypeStruct((rows, d), F32),
        compiler_params=_params("parallel"),
        name="xattn_layer",
    )(x, gain, w_q, kv, kv, w_o)


HALO = BF16_SUBLANES


def _ffn_in_kernel(x_ref, halo_ref, g_ref, wu_ref, wg_ref, cw_ref, cb_ref, o_ref, h_ref, gs_ref,
                   *, subs, blocks_per_seq):
    inside = pl.program_id(0) % blocks_per_seq != 0
    cw = cw_ref[...]
    cb = cb_ref[...]

    def row_blocks(normalise):
        start = 0
        for sub in subs:
            lo = HALO + start
            first = 0 if start == 0 else lo
            if normalise:
                h_ref[lo:lo + sub, :] = _rms(x_ref[start:start + sub, :], g_ref[...]).astype(BF16)
            gs_ref[first:lo + sub, :] = _dot(h_ref[first:lo + sub, :], wg_ref[...])
            u = _dot(h_ref[lo:lo + sub, :], wu_ref[...])
            conv = 0.0
            for tap in range(CONV_W):
                t0 = lo - (CONV_W - 1) + tap
                conv = conv + cw[tap:tap + 1] * gs_ref[t0:t0 + sub, :]
            gc = cb + conv
            o_ref[start:start + sub, :] = (gc * _sigmoid(gc) * u).astype(o_ref.dtype)
            start += sub

    @pl.when(pl.program_id(1) == 0)
    def _():
        hh = _rms(halo_ref[...], g_ref[...])
        h_ref[:HALO, :] = jnp.where(inside, hh, 0.0).astype(BF16)
        row_blocks(True)

    @pl.when(pl.program_id(1) != 0)
    def _():
        row_blocks(False)


def ffn_in(x, gain, w_in, layer, conv_w, conv_b, seq, tm, tn, subs):
    m, d = x.shape
    assert sum(subs) == tm
    f = w_in.shape[2] // 2
    nj = f // tn
    halo_per_block = tm // HALO
    return pl.pallas_call(
        functools.partial(_ffn_in_kernel, subs=subs, blocks_per_seq=seq // tm),
        grid=(m // tm, nj),
        in_specs=[
            pl.BlockSpec((tm, d), lambda i, j: (i, 0)),
            pl.BlockSpec((HALO, d), lambda i, j: (jnp.maximum(i * halo_per_block - 1, 0), 0)),
            pl.BlockSpec((1, d), lambda i, j: (0, 0)),
            pl.BlockSpec((None, d, tn), lambda i, j: (layer, 0, j)),
            pl.BlockSpec((None, d, tn), lambda i, j: (layer, 0, nj + j)),
            pl.BlockSpec((CONV_W, tn), lambda i, j: (0, j)),
            pl.BlockSpec((1, tn), lambda i, j: (0, j)),
        ],
        out_specs=pl.BlockSpec((tm, tn), lambda i, j: (i, j)),
        out_shape=jax.ShapeDtypeStruct((m, f), BF16),
        scratch_shapes=[pltpu.VMEM((HALO + tm, d), BF16), pltpu.VMEM((HALO + tm, tn), F32)],
        compiler_params=_params("parallel", "arbitrary"),
        name="ffn_in",
    )(x, x, gain, w_in, w_in, conv_w, conv_b)


def _norm_kernel(x_ref, g_ref, o_ref):
    o_ref[...] = _rms(x_ref[...], g_ref[...])


def rms_norm_rows(x, gain, tm):
    m, d = x.shape
    return pl.pallas_call(
        _norm_kernel,
        grid=(m // tm,),
        in_specs=[pl.BlockSpec((tm, d), lambda i: (i, 0)), pl.BlockSpec((1, d), lambda i: (0, 0))],
        out_specs=pl.BlockSpec((tm, d), lambda i: (i, 0)),
        out_shape=jax.ShapeDtypeStruct((m, d), F32),
        compiler_params=_params("parallel"),
        name="final_norm",
    )(x, gain)


def kernel(x, mem, mem_norm, norm_mix, norm_xattn, norm_ffn, ab_w_in, hgrn_lb_logits, hgrn_norm, gla_w_gk, gla_b_gk, gla_norm, ab_w_out, sb_w_qkv, sb_w_out, xa_w_q, xa_w_kv, xa_w_o, ffn_w_in, ffn_conv_w, ffn_conv_b, ffn_w_out, final_norm):
    b, s, d = x.shape
    depth = norm_mix.shape[0]
    m_len = mem.shape[1]
    rows = b * s
    tm = min(512, s)
    tm_big = min(1024, rows)
    tm_ffn = min(1024, s)
    ffn_subs = (tm_ffn // 2, tm_ffn // 2)
    ts = min(1024, s)
    tq_sb = min(256, s)

    def row(v):
        return v.reshape(1, -1)

    xr = x.reshape(rows, d)

    ab_main = 4 * HGRN_HEADS * HGRN_HEAD_DIM + GLA_HEADS * (2 * GLA_HEAD_K + 2 * GLA_HEAD_V)
    ab_w_in_b, ab_w_out_b = ab_w_in.astype(BF16), ab_w_out.astype(BF16)
    sb_w_qkv_b, sb_w_out_b = sb_w_qkv.astype(BF16), sb_w_out.astype(BF16)
    xa_w_q_b, xa_w_kv_b, xa_w_o_b = xa_w_q.astype(BF16), xa_w_kv.astype(BF16), xa_w_o.astype(BF16)
    ffn_w_in_b, ffn_w_out_b = ffn_w_in.astype(BF16), ffn_w_out.astype(BF16)

    mem_rows = mem.reshape(b * m_len, d)

    for layer in range(depth):
        gain = row(norm_mix[layer])
        if layer % 2 == 0:
            a = layer // 2
            w_low = jnp.pad(ab_w_in[a][:, ab_main:], ((0, 0), (0, LANES - GLA_GATE_RANK))).astype(BF16)
            w_gk = jnp.pad(gla_w_gk[a], ((0, LANES - GLA_GATE_RANK), (0, 0))).astype(BF16)
            p, gk = ab_proj(xr, gain, ab_w_in_b, a, ab_main, w_low, w_gk,
                            row(gla_b_gk[a]), tm=tm_big, tn=1024)
            p = p.reshape(b, s, ab_main)
            o_a = hgrn_mixer(p, hgrn_lb_logits, a, row(hgrn_norm[a]), ts)
            o_b = gla_mixer(p, gk.reshape(b, s, -1), row(gla_norm[a]), ts)
            xr = matmul2_resid(o_a.reshape(rows, -1), o_b.reshape(rows, -1),
                               ab_w_out_b, a, xr, tm=tm, tn=d)
        else:
            c = layer // 2
            qkv = norm_matmul(xr, gain, sb_w_qkv_b, c, BF16, tm=tm_big, tn=1024,
                              lead_cols=d, lead_scale=(SB_HEAD_DIM ** -0.5) * LOG2_E)
            o = sb_attention(qkv.reshape(b, s, -1), tq_sb, heads=SB_HEADS_PER_STEP)
            xr = matmul_resid(o.reshape(rows, d), sb_w_out_b, c, xr, tm=tm, tn=d)

        kv = norm_matmul(mem_rows, row(mem_norm), xa_w_kv_b, layer, BF16,
                         tm=min(1024, b * m_len), tn=d)
        xr = xattn_layer(xr, row(norm_xattn[layer]), xa_w_q_b, kv.reshape(b, m_len, 2 * d),
                         xa_w_o_b, layer, seq=s, tm=tm)

        act = ffn_in(xr, row(norm_ffn[layer]), ffn_w_in_b, layer, ffn_conv_w[layer],
                     row(ffn_conv_b[layer]), seq=s, tm=tm_ffn, tn=512, subs=ffn_subs)
        xr = matmul_resid(act, ffn_w_out_b, layer, xr, tm=tm_big, tn=512)

    return rms_norm_rows(xr, row(final_norm), tm).reshape(b, s, d)
```

```python
import functools
from typing import NamedTuple

import jax
import jax.numpy as jnp
from jax import lax
from jax.experimental import pallas as pl
from jax.experimental.pallas import tpu as pltpu

F32 = jnp.float32
BF16 = jnp.bfloat16

RMS_EPS = 1e-6
CHUNK = 64
HGRN_HEADS = 8
HGRN_HEAD_DIM = 128
GLA_HEADS = 4
GLA_HEAD_K = 128
GLA_HEAD_V = 256
GLA_GATE_RANK = 16
GLA_GATE_NORMALIZER = 16.0
SB_HEADS = 16
SB_HEAD_DIM = 128
XA_HEADS = 4
CONV_W = 3
LOG2_E = 1.4426950408889634
EXP2_CLAMP = 126.0
SB_HEADS_PER_STEP = 4

LANES = 128
BF16_SUBLANES = 16
VMEM_LIMIT_BYTES = 52 * 1024 * 1024
NORM_SUB_ROWS = 256


def _params(*semantics):
    return pltpu.CompilerParams(dimension_semantics=semantics, vmem_limit_bytes=VMEM_LIMIT_BYTES)


def _rms(x, gain):
    return x * lax.rsqrt(jnp.mean(x * x, axis=-1, keepdims=True) + RMS_EPS) * gain


def _sigmoid(x):
    return 1.0 / (1.0 + jnp.exp(-x))


def _softplus(x):
    return jnp.maximum(x, 0.0) + jnp.log(1.0 + jnp.exp(-jnp.abs(x)))


def _dot(a, b):
    return jnp.dot(a, b, preferred_element_type=F32)


def _dot_nt(a, b):
    return lax.dot_general(a, b, (((1,), (1,)), ((), ())), preferred_element_type=F32)


def _dot_tn(a, b):
    return lax.dot_general(a, b, (((0,), (0,)), ((), ())), preferred_element_type=F32)


def _split3(x):
    hi = x.astype(BF16)
    r1 = x - hi.astype(F32)
    mid = r1.astype(BF16)
    lo = (r1 - mid.astype(F32)).astype(BF16)
    return hi, mid, lo


class _Rider(NamedTuple):
    src: jax.Array
    layer: int
    row_blocks: int
    col_blocks: int


def _plan_rider(src, layer, grid):
    steps = 1
    for g in grid:
        steps *= g
    k, n = src.shape[-2:]
    for col_blocks in range(1, steps + 1):
        if steps % col_blocks or n % (col_blocks * LANES):
            continue
        row_blocks = steps // col_blocks
        if k % (row_blocks * BF16_SUBLANES) == 0:
            return _Rider(src, layer, row_blocks, col_blocks)
    return None


def _rider_specs(riders, grid):
    def step(*ids):
        lin = ids[0]
        for g, i in zip(grid[1:], ids[1:]):
            lin = lin * g + i
        return lin

    in_specs, out_specs, out_shapes = [], [], []
    for r in riders:
        k, n = r.src.shape[-2:]
        block = (k // r.row_blocks, n // r.col_blocks)

        def block_index(*ids, cb=r.col_blocks):
            lin = step(*ids)
            return lin // cb, lin % cb

        def src_index(*ids, layer=r.layer, f=block_index):
            return (layer,) + tuple(f(*ids))

        in_specs.append(pl.BlockSpec((None,) + block, src_index))
        out_specs.append(pl.BlockSpec(block, block_index))
        out_shapes.append(jax.ShapeDtypeStruct((k, n), BF16))
    return in_specs, out_specs, out_shapes


def _cast_riders(src_refs, dst_refs):
    for src_ref, dst_ref in zip(src_refs, dst_refs):
        dst_ref[...] = src_ref[...].astype(BF16)


def _row_blocks(total, size):
    return [slice(r, r + size) for r in range(0, total, size)]


def _norm_matmul_kernel(x_ref, g_ref, w_ref, o_ref, h_ref, *, lead_blocks, lead_scale, sub):
    j = pl.program_id(1)

    def project(h):
        acc = _dot(h, w_ref[...])
        if lead_blocks:
            acc = acc * jnp.where(j < lead_blocks, lead_scale, 1.0)
        return acc.astype(o_ref.dtype)

    @pl.when(j == 0)
    def _():
        for rows in _row_blocks(x_ref.shape[0], sub):
            h = _rms(x_ref[rows, :], g_ref[...]).astype(BF16)
            h_ref[rows, :] = h
            o_ref[rows, :] = project(h)

    @pl.when(j != 0)
    def _():
        o_ref[...] = project(h_ref[...])


def norm_matmul(x, gain, w, layer, out_dtype, tm, tn, lead_cols=0, lead_scale=1.0):
    m, k = x.shape
    n = w.shape[2]
    assert lead_cols % tn == 0
    return pl.pallas_call(
        functools.partial(_norm_matmul_kernel, lead_blocks=lead_cols // tn, lead_scale=lead_scale,
                          sub=min(NORM_SUB_ROWS, tm)),
        grid=(m // tm, n // tn),
        in_specs=[
            pl.BlockSpec((tm, k), lambda i, j: (i, 0)),
            pl.BlockSpec((1, k), lambda i, j: (0, 0)),
            pl.BlockSpec((None, k, tn), lambda i, j: (layer, 0, j)),
        ],
        out_specs=pl.BlockSpec((tm, tn), lambda i, j: (i, j)),
        out_shape=jax.ShapeDtypeStruct((m, n), out_dtype),
        scratch_shapes=[pltpu.VMEM((tm, k), BF16)],
        compiler_params=_params("parallel", "arbitrary"),
        name="norm_matmul",
    )(x, gain, w)


def _ab_proj_kernel(x_ref, g_ref, w_ref, wlow_ref, wgk_ref, bgk_ref, o_ref, gk_ref, h_ref, *, sub):
    @pl.when(pl.program_id(1) == 0)
    def _():
        for rows in _row_blocks(x_ref.shape[0], sub):
            h = _rms(x_ref[rows, :], g_ref[...]).astype(BF16)
            h_ref[rows, :] = h
            o_ref[rows, :] = _dot(h, w_ref[...])
            low = _dot(h, wlow_ref[...])
            pre = _dot(low.astype(BF16), wgk_ref[...]) + bgk_ref[...]
            gk_ref[rows, :] = -_softplus(-pre) * (1.0 / GLA_GATE_NORMALIZER)

    @pl.when(pl.program_id(1) != 0)
    def _():
        o_ref[...] = _dot(h_ref[...], w_ref[...])


def ab_proj(x, gain, w, layer, n, w_low, w_gk, b_gk, tm, tn):
    m, k = x.shape
    assert n % tn == 0
    ngk = w_gk.shape[1]
    return pl.pallas_call(
        functools.partial(_ab_proj_kernel, sub=min(NORM_SUB_ROWS, tm)),
        grid=(m // tm, n // tn),
        in_specs=[
            pl.BlockSpec((tm, k), lambda i, j: (i, 0)),
            pl.BlockSpec((1, k), lambda i, j: (0, 0)),
            pl.BlockSpec((None, k, tn), lambda i, j: (layer, 0, j)),
            pl.BlockSpec((k, LANES), lambda i, j: (0, 0)),
            pl.BlockSpec((LANES, ngk), lambda i, j: (0, 0)),
            pl.BlockSpec((1, ngk), lambda i, j: (0, 0)),
        ],
        out_specs=[
            pl.BlockSpec((tm, tn), lambda i, j: (i, j)),
            pl.BlockSpec((tm, ngk), lambda i, j: (i, 0)),
        ],
        out_shape=[
            jax.ShapeDtypeStruct((m, n), F32),
            jax.ShapeDtypeStruct((m, ngk), F32),
        ],
        scratch_shapes=[pltpu.VMEM((tm, k), BF16)],
        compiler_params=_params("parallel", "arbitrary"),
        name="ab_proj",
    )(x, gain, w, w_low, w_gk, b_gk)


def _matmul_resid_kernel(a_ref, w_ref, r_ref, o_ref):
    o_ref[...] = r_ref[...] + _dot(a_ref[...], w_ref[...])


def matmul_resid(a, w, layer, resid, tm, tn):
    m, k = a.shape
    n = w.shape[2]
    return pl.pallas_call(
        _matmul_resid_kernel,
        grid=(m // tm, n // tn),
        in_specs=[
            pl.BlockSpec((tm, k), lambda i, j: (i, 0)),
            pl.BlockSpec((None, k, tn), lambda i, j: (layer, 0, j)),
            pl.BlockSpec((tm, tn), lambda i, j: (i, j)),
        ],
        out_specs=pl.BlockSpec((tm, tn), lambda i, j: (i, j)),
        out_shape=jax.ShapeDtypeStruct((m, n), F32),
        compiler_params=_params("parallel", "arbitrary"),
        name="matmul_resid",
    )(a, w, resid)


def _matmul2_resid_kernel(a1_ref, a2_ref, w1_ref, w2_ref, r_ref, o_ref):
    o_ref[...] = r_ref[...] + (_dot(a1_ref[...], w1_ref[...]) + _dot(a2_ref[...], w2_ref[...]))


def matmul2_resid(a1, a2, w, layer, resid, tm, tn):
    m, kh = a1.shape
    n = w.shape[2]
    return pl.pallas_call(
        _matmul2_resid_kernel,
        grid=(m // tm, n // tn),
        in_specs=[
            pl.BlockSpec((tm, kh), lambda i, j: (i, 0)),
            pl.BlockSpec((tm, kh), lambda i, j: (i, 0)),
            pl.BlockSpec((None, kh, tn), lambda i, j: (layer, 0, j)),
            pl.BlockSpec((None, kh, tn), lambda i, j: (layer, 1, j)),
            pl.BlockSpec((tm, tn), lambda i, j: (i, j)),
        ],
        out_specs=pl.BlockSpec((tm, tn), lambda i, j: (i, j)),
        out_shape=jax.ShapeDtypeStruct((m, n), F32),
        compiler_params=_params("parallel", "arbitrary"),
        name="matmul2_resid",
    )(a1, a2, w, w, resid)


def _linear_attention_block(q, k, v, g, state_ref):
    n = q.shape[0] // CHUNK
    row = lax.broadcasted_iota(jnp.int32, (CHUNK, CHUNK), 0)
    col = lax.broadcasted_iota(jnp.int32, (CHUNK, CHUNK), 1)
    causal = row >= col
    tri = causal.astype(BF16)
    sl = [slice(c * CHUNK, (c + 1) * CHUNK) for c in range(n)]

    g_terms = [_split3(g[s]) for s in sl]
    b = [_dot(tri, hi) + _dot(tri, mid) + _dot(tri, lo) for hi, mid, lo in g_terms]
    b_mid = [x[CHUNK // 2 - 1:CHUNK // 2] for x in b]
    b_last = [x[CHUNK - 1:] for x in b]
    vb = [v[s].astype(BF16) for s in sl]
    qs = [(q[s] * jnp.exp(x - m)).astype(BF16) for s, x, m in zip(sl, b, b_mid)]
    ks = [(k[s] * jnp.exp(m - x)).astype(BF16) for s, x, m in zip(sl, b, b_mid)]
    kd = [(k[s] * jnp.exp(l - x)).astype(BF16) for s, x, l in zip(sl, b, b_last)]
    qd = [(q[s] * jnp.exp(x)).astype(BF16) for s, x in zip(sl, b)]

    scores = [jnp.where(causal, _dot_nt(a, c), 0.0).astype(BF16) for a, c in zip(qs, ks)]
    update = [_dot_tn(a, c) for a, c in zip(vb, kd)]
    o_intra = [_dot(a, c) for a, c in zip(scores, vb)]

    state = state_ref[...]
    states = []
    for c in range(n):
        states.append(state.astype(BF16))
        state = state * jnp.exp(b_last[c]) + update[c]
    state_ref[...] = state

    o = [oi + _dot_nt(a, st) for oi, a, st in zip(o_intra, qd, states)]
    return jnp.concatenate(o, axis=0)


def _hgrn_kernel(*refs, lb_rows, n_riders):
    aq_ref, af_ref, ai_ref, ag_ref, lbl_ref, gain_ref = refs[:6]
    o_ref, state_ref = refs[6 + n_riders], refs[-1]
    _cast_riders(refs[6:6 + n_riders], refs[7 + n_riders:-1])

    @pl.when(pl.program_id(2) == 0)
    def _():
        state_ref[...] = jnp.zeros_like(state_ref)

    logits = lbl_ref[...]
    e = jnp.exp(logits - jnp.max(logits, axis=0, keepdims=True))
    lb = jnp.sum(e[:lb_rows], axis=0, keepdims=True) / jnp.sum(e, axis=0, keepdims=True)

    a_q = aq_ref[0]
    f = lb + (1.0 - lb) * _sigmoid(af_ref[0])
    o = _linear_attention_block(a_q * _sigmoid(a_q), 1.0 - f, ai_ref[0], jnp.log(f), state_ref)
    o_ref[0] = (_rms(o, gain_ref[...]) * _sigmoid(ag_ref[0])).astype(o_ref.dtype)


def hgrn_mixer(p, lb_logits, a_idx, gain, ts, riders=()):
    b, s, _ = p.shape
    nh, hd = HGRN_HEADS, HGRN_HEAD_DIM
    nlb = lb_logits.shape[0]
    grid = (b, nh, s // ts)
    r_in, r_out, r_shapes = _rider_specs(riders, grid)

    def col(part):
        return pl.BlockSpec((1, ts, hd), lambda bi, h, si: (bi, si, part * nh + h))

    o, *cast = pl.pallas_call(
        functools.partial(_hgrn_kernel, lb_rows=a_idx + 1, n_riders=len(riders)),
        grid=grid,
        in_specs=[
            col(0), col(1), col(2), col(3),
            pl.BlockSpec((nlb, hd), lambda bi, h, si: (0, h)),
            pl.BlockSpec((1, hd), lambda bi, h, si: (0, 0)),
        ] + r_in,
        out_specs=[pl.BlockSpec((1, ts, hd), lambda bi, h, si: (bi, si, h))] + r_out,
        out_shape=[jax.ShapeDtypeStruct((b, s, nh * hd), BF16)] + r_shapes,
        scratch_shapes=[pltpu.VMEM((hd, hd), F32)],
        compiler_params=_params("parallel", "parallel", "arbitrary"),
        name="hgrn_mixer",
    )(p, p, p, p, lb_logits, gain, *[r.src for r in riders])
    return o, cast


def _gla_kernel(q_ref, k_ref, v_ref, gg_ref, gk_ref, gain_ref, o_ref, state_ref):
    @pl.when(pl.program_id(2) == 0)
    def _():
        state_ref[...] = jnp.zeros_like(state_ref)

    q = q_ref[0] * (GLA_HEAD_K ** -0.5)
    o = _linear_attention_block(q, k_ref[0], v_ref[0], gk_ref[0], state_ref)
    g_g = gg_ref[0]
    o_ref[0] = (_rms(o, gain_ref[...]) * (g_g * _sigmoid(g_g))).astype(o_ref.dtype)


def gla_mixer(p, gk, gain, ts):
    b, s, _ = p.shape
    nh, kd, vd = GLA_HEADS, GLA_HEAD_K, GLA_HEAD_V
    base = 4 * HGRN_HEADS * HGRN_HEAD_DIM
    q0 = base // kd
    k0 = q0 + nh
    v0 = (base + 2 * nh * kd) // vd
    g0 = v0 + nh
    return pl.pallas_call(
        _gla_kernel,
        grid=(b, nh, s // ts),
        in_specs=[
            pl.BlockSpec((1, ts, kd), lambda bi, h, si: (bi, si, q0 + h)),
            pl.BlockSpec((1, ts, kd), lambda bi, h, si: (bi, si, k0 + h)),
            pl.BlockSpec((1, ts, vd), lambda bi, h, si: (bi, si, v0 + h)),
            pl.BlockSpec((1, ts, vd), lambda bi, h, si: (bi, si, g0 + h)),
            pl.BlockSpec((1, ts, kd), lambda bi, h, si: (bi, si, h)),
            pl.BlockSpec((1, vd), lambda bi, h, si: (0, 0)),
        ],
        out_specs=pl.BlockSpec((1, ts, vd), lambda bi, h, si: (bi, si, h)),
        out_shape=jax.ShapeDtypeStruct((b, s, nh * vd), BF16),
        scratch_shapes=[pltpu.VMEM((vd, kd), F32)],
        compiler_params=_params("parallel", "parallel", "arbitrary"),
        name="gla_mixer",
    )(p, p, p, p, gk, gain)


def _sb_kernel(*refs, tq, heads, n_riders):
    q_ref, k_ref, v_ref = refs[:3]
    o_ref = refs[3 + n_riders]
    _cast_riders(refs[3:3 + n_riders], refs[4 + n_riders:])

    qi = pl.program_id(2)
    hd = SB_HEAD_DIM
    row = lax.broadcasted_iota(jnp.int32, (tq, tq), 0)
    col = lax.broadcasted_iota(jnp.int32, (tq, tq), 1)
    behind_ones = (row > col).astype(BF16)
    strictly_causal = col < row
    head_cols = [slice(h * hd, (h + 1) * hd) for h in range(heads)]

    def key_block(j, carry, diagonal):
        start = pl.multiple_of(j * tq, tq)
        acc = [c[0] for c in carry]
        later = [c[1] for c in carry]
        z2 = [_dot_nt(q_ref[0, :, hc], k_ref[0, pl.ds(start, tq), hc]) for hc in head_cols]
        sp2 = [jnp.maximum(z, jnp.log(1.0 + jnp.exp2(jnp.minimum(z, EXP2_CLAMP))) * LOG2_E)
               for z in z2]
        if diagonal:
            sp2 = [jnp.where(strictly_causal, x, 0.0) for x in sp2]
        own = [z - x - lt for z, x, lt in zip(z2, sp2, later)]
        behind = [_dot(x.astype(BF16), behind_ones) for x in sp2]
        log2_a = [o - bh for o, bh in zip(own, behind)]
        if diagonal:
            log2_a = [jnp.where(strictly_causal, x, -jnp.inf) for x in log2_a]
        att = [jnp.exp2(x).astype(BF16) for x in log2_a]
        acc = [a + _dot(p, v_ref[0, pl.ds(start, tq), hc]) for a, p, hc in zip(acc, att, head_cols)]
        later = [lt + (x[:, 0:1] + bh[:, 0:1]) for lt, x, bh in zip(later, sp2, behind)]
        return tuple(zip(acc, later))

    zero = (jnp.zeros((tq, hd), F32), jnp.zeros((tq, 1), F32))
    carry = key_block(qi, (zero,) * heads, True)
    carry = lax.fori_loop(0, qi, lambda t, c: key_block(qi - 1 - t, c, False), carry)
    for h, hc in enumerate(head_cols):
        o_ref[0, :, hc] = carry[h][0].astype(o_ref.dtype)


def sb_attention(qkv, tq, heads, riders=()):
    b, s, _ = qkv.shape
    ng = SB_HEADS // heads
    w = heads * SB_HEAD_DIM
    grid = (b, ng, s // tq)
    r_in, r_out, r_shapes = _rider_specs(riders, grid)
    o, *cast = pl.pallas_call(
        functools.partial(_sb_kernel, tq=tq, heads=heads, n_riders=len(riders)),
        grid=grid,
        in_specs=[
            pl.BlockSpec((1, tq, w), lambda bi, g, qi: (bi, qi, g)),
            pl.BlockSpec((1, s, w), lambda bi, g, qi: (bi, 0, ng + g)),
            pl.BlockSpec((1, s, w), lambda bi, g, qi: (bi, 0, 2 * ng + g)),
        ] + r_in,
        out_specs=[pl.BlockSpec((1, tq, w), lambda bi, g, qi: (bi, qi, g))] + r_out,
        out_shape=[jax.ShapeDtypeStruct((b, s, SB_HEADS * SB_HEAD_DIM), BF16)] + r_shapes,
        compiler_params=_params("parallel", "parallel", "arbitrary"),
        name="sb_attention",
    )(qkv, qkv, qkv, *[r.src for r in riders])
    return o, cast


def _xattn_layer_kernel(x_ref, g_ref, wq_ref, k_ref, v_ref, wo_ref, o_ref):
    x = x_ref[...]
    d = x.shape[-1]
    hd = d // XA_HEADS
    head_cols = [slice(h * hd, (h + 1) * hd) for h in range(XA_HEADS)]
    h_in = [_rms(x[rows], g_ref[...]).astype(BF16)
            for rows in _row_blocks(x.shape[0], min(NORM_SUB_ROWS, x.shape[0]))]
    q = [jnp.concatenate([_dot(hp, wq_ref[:, hc]).astype(BF16) for hp in h_in], axis=0)
         for hc in head_cols]
    scores = [_dot_nt(qh, k_ref[0, :, hc]) * (hd ** -0.5) for qh, hc in zip(q, head_cols)]
    e = [jnp.exp(sc - jnp.max(sc, axis=-1, keepdims=True)) for sc in scores]
    probs = [(eh / jnp.sum(eh, axis=-1, keepdims=True)).astype(BF16) for eh in e]
    o = [_dot(ph, v_ref[0, :, hc]).astype(BF16) for ph, hc in zip(probs, head_cols)]
    o_ref[...] = x + _dot(jnp.concatenate(o, axis=1), wo_ref[...])


def xattn_layer(x, gain, w_q, kv, w_o, layer, seq, tm):
    rows, d = x.shape
    m = kv.shape[1]
    blocks_per_seq = seq // tm
    whole = functools.partial(pl.BlockSpec, (None, d, d), lambda i: (layer, 0, 0),
                              pipeline_mode=pl.Buffered(1))
    return pl.pallas_call(
        _xattn_layer_kernel,
        grid=(rows // tm,),
        in_specs=[
            pl.BlockSpec((tm, d), lambda i: (i, 0)),
            pl.BlockSpec((1, d), lambda i: (0, 0)),
            whole(),
            pl.BlockSpec((1, m, d), lambda i: (i // blocks_per_seq, 0, 0)),
            pl.BlockSpec((1, m, d), lambda i: (i // blocks_per_seq, 0, 1)),
            whole(),
        ],
        out_specs=pl.BlockSpec((tm, d), lambda i: (i, 0)),
        out_shape=jax.ShapeDtypeStruct((rows, d), F32),
        compiler_params=_params("parallel"),
        name="xattn_layer",
    )(x, gain, w_q, kv, kv, w_o)


HALO = BF16_SUBLANES


def _ffn_in_kernel(*refs, subs, blocks_per_seq, n_riders):
    x_ref, halo_ref, g_ref, wu_ref, wg_ref, cw_ref, cb_ref = refs[:7]
    o_ref, h_ref, gs_ref = refs[7 + n_riders], refs[-2], refs[-1]
    _cast_riders(refs[7:7 + n_riders], refs[8 + n_riders:-2])

    inside = pl.program_id(0) % blocks_per_seq != 0
    cw = cw_ref[...]
    cb = cb_ref[...]

    def row_blocks(normalise):
        start = 0
        for sub in subs:
            lo = HALO + start
            first = 0 if start == 0 else lo
            if normalise:
                h_ref[lo:lo + sub, :] = _rms(x_ref[start:start + sub, :], g_ref[...]).astype(BF16)
            gs_ref[first:lo + sub, :] = _dot(h_ref[first:lo + sub, :], wg_ref[...])
            u = _dot(h_ref[lo:lo + sub, :], wu_ref[...])
            conv = 0.0
            for tap in range(CONV_W):
                t0 = lo - (CONV_W - 1) + tap
                conv = conv + cw[tap:tap + 1] * gs_ref[t0:t0 + sub, :]
            gc = cb + conv
            o_ref[start:start + sub, :] = (gc * _sigmoid(gc) * u).astype(o_ref.dtype)
            start += sub

    @pl.when(pl.program_id(1) == 0)
    def _():
        hh = _rms(halo_ref[...], g_ref[...])
        h_ref[:HALO, :] = jnp.where(inside, hh, 0.0).astype(BF16)
        row_blocks(True)

    @pl.when(pl.program_id(1) != 0)
    def _():
        row_blocks(False)


def ffn_in_grid(m, f, tm, tn):
    return (m // tm, f // tn)


def ffn_in(x, gain, w_in, layer, conv_w, conv_b, seq, tm, tn, subs, riders=()):
    m, d = x.shape
    assert sum(subs) == tm
    f = w_in.shape[2] // 2
    grid = ffn_in_grid(m, f, tm, tn)
    nj = grid[1]
    halo_per_block = tm // HALO
    r_in, r_out, r_shapes = _rider_specs(riders, grid)
    o, *cast = pl.pallas_call(
        functools.partial(_ffn_in_kernel, subs=subs, blocks_per_seq=seq // tm,
                          n_riders=len(riders)),
        grid=grid,
        in_specs=[
            pl.BlockSpec((tm, d), lambda i, j: (i, 0)),
            pl.BlockSpec((HALO, d), lambda i, j: (jnp.maximum(i * halo_per_block - 1, 0), 0)),
            pl.BlockSpec((1, d), lambda i, j: (0, 0)),
            pl.BlockSpec((None, d, tn), lambda i, j: (layer, 0, j)),
            pl.BlockSpec((None, d, tn), lambda i, j: (layer, 0, nj + j)),
            pl.BlockSpec((CONV_W, tn), lambda i, j: (0, j)),
            pl.BlockSpec((1, tn), lambda i, j: (0, j)),
        ] + r_in,
        out_specs=[pl.BlockSpec((tm, tn), lambda i, j: (i, j))] + r_out,
        out_shape=[jax.ShapeDtypeStruct((m, f), BF16)] + r_shapes,
        scratch_shapes=[pltpu.VMEM((HALO + tm, d), BF16), pltpu.VMEM((HALO + tm, tn), F32)],
        compiler_params=_params("parallel", "arbitrary"),
        name="ffn_in",
    )(x, x, gain, w_in, w_in, conv_w, conv_b, *[r.src for r in riders])
    return o, cast


def _norm_kernel(x_ref, g_ref, o_ref):
    o_ref[...] = _rms(x_ref[...], g_ref[...])


def rms_norm_rows(x, gain, tm):
    m, d = x.shape
    return pl.pallas_call(
        _norm_kernel,
        grid=(m // tm,),
        in_specs=[pl.BlockSpec((tm, d), lambda i: (i, 0)), pl.BlockSpec((1, d), lambda i: (0, 0))],
        out_specs=pl.BlockSpec((tm, d), lambda i: (i, 0)),
        out_shape=jax.ShapeDtypeStruct((m, d), F32),
        compiler_params=_params("parallel"),
        name="final_norm",
    )(x, gain)


def kernel(x, mem, mem_norm, norm_mix, norm_xattn, norm_ffn, ab_w_in, hgrn_lb_logits, hgrn_norm, gla_w_gk, gla_b_gk, gla_norm, ab_w_out, sb_w_qkv, sb_w_out, xa_w_q, xa_w_kv, xa_w_o, ffn_w_in, ffn_conv_w, ffn_conv_b, ffn_w_out, final_norm):
    b, s, d = x.shape
    depth = norm_mix.shape[0]
    m_len = mem.shape[1]
    rows = b * s
    tm = min(512, s)
    tm_big = min(1024, rows)
    tm_ffn = min(1024, s)
    ffn_subs = (tm_ffn // 2, tm_ffn // 2)
    ts = min(1024, s)
    tq_sb = min(256, s)

    def row(v):
        return v.reshape(1, -1)

    xr = x.reshape(rows, d)

    stacks = dict(ab_w_in=ab_w_in, ab_w_out=ab_w_out, sb_w_qkv=sb_w_qkv, sb_w_out=sb_w_out,
                  xa_w_q=xa_w_q, xa_w_kv=xa_w_kv, xa_w_o=xa_w_o, ffn_w_in=ffn_w_in,
                  ffn_w_out=ffn_w_out)
    ready = {}

    def weight(name, idx):
        if (name, idx) not in ready:
            ready[(name, idx)] = stacks[name][idx:idx + 1].astype(BF16)
        return ready[(name, idx)]

    def plan(wanted, grid):
        keys = [key for key in wanted if key not in ready]
        riders = [_plan_rider(stacks[name], idx, grid) for name, idx in keys]
        keys = [key for key, r in zip(keys, riders) if r is not None]
        return [r for r in riders if r is not None], keys

    def adopt(keys, cast):
        for key, w_bf16 in zip(keys, cast):
            ready[key] = w_bf16[None]

    ab_main = 4 * HGRN_HEADS * HGRN_HEAD_DIM + GLA_HEADS * (2 * GLA_HEAD_K + 2 * GLA_HEAD_V)
    mem_rows = mem.reshape(b * m_len, d)
    d_ff = ffn_w_in.shape[2] // 2

    for layer in range(depth):
        gain = row(norm_mix[layer])
        xattn_weights = [("xa_w_q", layer), ("xa_w_kv", layer), ("xa_w_o", layer)]
        if layer % 2 == 0:
            a = layer // 2
            w_low = jnp.pad(ab_w_in[a][:, ab_main:], ((0, 0), (0, LANES - GLA_GATE_RANK))).astype(BF16)
            w_gk = jnp.pad(gla_w_gk[a], ((0, LANES - GLA_GATE_RANK), (0, 0))).astype(BF16)
            p, gk = ab_proj(xr, gain, weight("ab_w_in", a), 0, ab_main, w_low, w_gk,
                            row(gla_b_gk[a]), tm=tm_big, tn=1024)
            p = p.reshape(b, s, ab_main)
            riders, keys = plan([("ab_w_out", a)] + xattn_weights, (b, HGRN_HEADS, s // ts))
            o_a, cast = hgrn_mixer(p, hgrn_lb_logits, a, row(hgrn_norm[a]), ts, riders)
            adopt(keys, cast)
            o_b = gla_mixer(p, gk.reshape(b, s, -1), row(gla_norm[a]), ts)
            xr = matmul2_resid(o_a.reshape(rows, -1), o_b.reshape(rows, -1),
                               weight("ab_w_out", a), 0, xr, tm=tm, tn=d)
        else:
            c = layer // 2
            qkv = norm_matmul(xr, gain, weight("sb_w_qkv", c), 0, BF16, tm=tm_big, tn=1024,
                              lead_cols=d, lead_scale=(SB_HEAD_DIM ** -0.5) * LOG2_E)
            riders, keys = plan([("sb_w_out", c)] + xattn_weights
                                + [("ffn_w_in", layer), ("ffn_w_out", layer)],
                                (b, SB_HEADS // SB_HEADS_PER_STEP, s // tq_sb))
            o, cast = sb_attention(qkv.reshape(b, s, -1), tq_sb, SB_HEADS_PER_STEP, riders)
            adopt(keys, cast)
            xr = matmul_resid(o.reshape(rows, d), weight("sb_w_out", c), 0, xr, tm=tm, tn=d)

        kv = norm_matmul(mem_rows, row(mem_norm), weight("xa_w_kv", layer), 0, BF16,
                         tm=min(1024, b * m_len), tn=d)
        xr = xattn_layer(xr, row(norm_xattn[layer]), weight("xa_w_q", layer),
                         kv.reshape(b, m_len, 2 * d), weight("xa_w_o", layer), 0, seq=s, tm=tm)

        riders, keys = plan([("ffn_w_out", layer)], ffn_in_grid(rows, d_ff, tm_ffn, 512))
        act, cast = ffn_in(xr, row(norm_ffn[layer]), weight("ffn_w_in", layer), 0,
                           ffn_conv_w[layer], row(ffn_conv_b[layer]), seq=s, tm=tm_ffn, tn=512,
                           subs=ffn_subs, riders=riders)
        adopt(keys, cast)
        xr = matmul_resid(act, weight("ffn_w_out", layer), 0, xr, tm=tm_big, tn=512)

    return rms_norm_rows(xr, row(final_norm), tm).reshape(b, s, d)
```

```python
import functools
from typing import NamedTuple

import jax
import jax.numpy as jnp
from jax import lax
from jax.experimental import pallas as pl
from jax.experimental.pallas import tpu as pltpu

F32 = jnp.float32
BF16 = jnp.bfloat16

RMS_EPS = 1e-6
CHUNK = 64
HGRN_HEADS = 8
HGRN_HEAD_DIM = 128
GLA_HEADS = 4
GLA_HEAD_K = 128
GLA_HEAD_V = 256
GLA_GATE_RANK = 16
GLA_GATE_NORMALIZER = 16.0
SB_HEADS = 16
SB_HEAD_DIM = 128
XA_HEADS = 4
CONV_W = 3
LOG2_E = 1.4426950408889634
EXP2_CLAMP = 126.0
SB_HEADS_PER_STEP = 4

LANES = 128
BF16_SUBLANES = 16
VMEM_LIMIT_BYTES = 52 * 1024 * 1024
NORM_SUB_ROWS = 256


def _params(*semantics):
    return pltpu.CompilerParams(dimension_semantics=semantics, vmem_limit_bytes=VMEM_LIMIT_BYTES)


def _rms(x, gain):
    return x * lax.rsqrt(jnp.mean(x * x, axis=-1, keepdims=True) + RMS_EPS) * gain


def _sigmoid(x):
    return 1.0 / (1.0 + jnp.exp(-x))


def _softplus(x):
    return jnp.maximum(x, 0.0) + jnp.log(1.0 + jnp.exp(-jnp.abs(x)))


def _dot(a, b):
    return jnp.dot(a, b, preferred_element_type=F32)


def _dot_nt(a, b):
    return lax.dot_general(a, b, (((1,), (1,)), ((), ())), preferred_element_type=F32)


def _dot_tn(a, b):
    return lax.dot_general(a, b, (((0,), (0,)), ((), ())), preferred_element_type=F32)


def _split3(x):
    hi = x.astype(BF16)
    r1 = x - hi.astype(F32)
    mid = r1.astype(BF16)
    lo = (r1 - mid.astype(F32)).astype(BF16)
    return hi, mid, lo


class _Rider(NamedTuple):
    src: jax.Array
    layer: int
    row_blocks: int
    col_blocks: int


def _plan_rider(src, layer, grid):
    steps = 1
    for g in grid:
        steps *= g
    k, n = src.shape[-2:]
    for col_blocks in range(1, steps + 1):
        if steps % col_blocks or n % (col_blocks * LANES):
            continue
        row_blocks = steps // col_blocks
        if k % (row_blocks * BF16_SUBLANES) == 0:
            return _Rider(src, layer, row_blocks, col_blocks)
    return None


def _rider_specs(riders, grid):
    def step(*ids):
        lin = ids[0]
        for g, i in zip(grid[1:], ids[1:]):
            lin = lin * g + i
        return lin

    in_specs, out_specs, out_shapes = [], [], []
    for r in riders:
        k, n = r.src.shape[-2:]
        block = (k // r.row_blocks, n // r.col_blocks)

        def block_index(*ids, cb=r.col_blocks):
            lin = step(*ids)
            return lin // cb, lin % cb

        def src_index(*ids, layer=r.layer, f=block_index):
            return (layer,) + tuple(f(*ids))

        in_specs.append(pl.BlockSpec((None,) + block, src_index))
        out_specs.append(pl.BlockSpec(block, block_index))
        out_shapes.append(jax.ShapeDtypeStruct((k, n), BF16))
    return in_specs, out_specs, out_shapes


def _cast_riders(src_refs, dst_refs):
    for src_ref, dst_ref in zip(src_refs, dst_refs):
        dst_ref[...] = src_ref[...].astype(BF16)


def _row_blocks(total, size):
    return [slice(r, r + size) for r in range(0, total, size)]


def _norm_matmul_kernel(x_ref, g_ref, w_ref, o_ref, h_ref, *, lead_blocks, lead_scale, sub):
    j = pl.program_id(1)

    def project(h):
        acc = _dot(h, w_ref[...])
        if lead_blocks:
            acc = acc * jnp.where(j < lead_blocks, lead_scale, 1.0)
        return acc.astype(o_ref.dtype)

    @pl.when(j == 0)
    def _():
        for rows in _row_blocks(x_ref.shape[0], sub):
            h = _rms(x_ref[rows, :], g_ref[...]).astype(BF16)
            h_ref[rows, :] = h
            o_ref[rows, :] = project(h)

    @pl.when(j != 0)
    def _():
        o_ref[...] = project(h_ref[...])


def norm_matmul(x, gain, w, layer, out_dtype, tm, tn, lead_cols=0, lead_scale=1.0):
    m, k = x.shape
    n = w.shape[2]
    assert lead_cols % tn == 0
    return pl.pallas_call(
        functools.partial(_norm_matmul_kernel, lead_blocks=lead_cols // tn, lead_scale=lead_scale,
                          sub=min(NORM_SUB_ROWS, tm)),
        grid=(m // tm, n // tn),
        in_specs=[
            pl.BlockSpec((tm, k), lambda i, j: (i, 0)),
            pl.BlockSpec((1, k), lambda i, j: (0, 0)),
            pl.BlockSpec((None, k, tn), lambda i, j: (layer, 0, j)),
        ],
        out_specs=pl.BlockSpec((tm, tn), lambda i, j: (i, j)),
        out_shape=jax.ShapeDtypeStruct((m, n), out_dtype),
        scratch_shapes=[pltpu.VMEM((tm, k), BF16)],
        compiler_params=_params("parallel", "arbitrary"),
        name="norm_matmul",
    )(x, gain, w)


def _ab_proj_kernel(x_ref, g_ref, w_ref, wlow_ref, wgk_ref, bgk_ref, o_ref, gk_ref, h_ref, *, sub):
    @pl.when(pl.program_id(1) == 0)
    def _():
        for rows in _row_blocks(x_ref.shape[0], sub):
            h = _rms(x_ref[rows, :], g_ref[...]).astype(BF16)
            h_ref[rows, :] = h
            o_ref[rows, :] = _dot(h, w_ref[...])
            low = _dot(h, wlow_ref[...])
            pre = _dot(low.astype(BF16), wgk_ref[...]) + bgk_ref[...]
            gk_ref[rows, :] = -_softplus(-pre) * (LOG2_E / GLA_GATE_NORMALIZER)

    @pl.when(pl.program_id(1) != 0)
    def _():
        o_ref[...] = _dot(h_ref[...], w_ref[...])


def ab_proj(x, gain, w, layer, n, w_low, w_gk, b_gk, tm, tn):
    m, k = x.shape
    assert n % tn == 0
    ngk = w_gk.shape[1]
    return pl.pallas_call(
        functools.partial(_ab_proj_kernel, sub=min(NORM_SUB_ROWS, tm)),
        grid=(m // tm, n // tn),
        in_specs=[
            pl.BlockSpec((tm, k), lambda i, j: (i, 0)),
            pl.BlockSpec((1, k), lambda i, j: (0, 0)),
            pl.BlockSpec((None, k, tn), lambda i, j: (layer, 0, j)),
            pl.BlockSpec((k, LANES), lambda i, j: (0, 0)),
            pl.BlockSpec((LANES, ngk), lambda i, j: (0, 0)),
            pl.BlockSpec((1, ngk), lambda i, j: (0, 0)),
        ],
        out_specs=[
            pl.BlockSpec((tm, tn), lambda i, j: (i, j)),
            pl.BlockSpec((tm, ngk), lambda i, j: (i, 0)),
        ],
        out_shape=[
            jax.ShapeDtypeStruct((m, n), F32),
            jax.ShapeDtypeStruct((m, ngk), F32),
        ],
        scratch_shapes=[pltpu.VMEM((tm, k), BF16)],
        compiler_params=_params("parallel", "arbitrary"),
        name="ab_proj",
    )(x, gain, w, w_low, w_gk, b_gk)


def _matmul_resid_kernel(a_ref, w_ref, r_ref, o_ref):
    o_ref[...] = r_ref[...] + _dot(a_ref[...], w_ref[...])


def matmul_resid(a, w, layer, resid, tm, tn):
    m, k = a.shape
    n = w.shape[2]
    return pl.pallas_call(
        _matmul_resid_kernel,
        grid=(m // tm, n // tn),
        in_specs=[
            pl.BlockSpec((tm, k), lambda i, j: (i, 0)),
            pl.BlockSpec((None, k, tn), lambda i, j: (layer, 0, j)),
            pl.BlockSpec((tm, tn), lambda i, j: (i, j)),
        ],
        out_specs=pl.BlockSpec((tm, tn), lambda i, j: (i, j)),
        out_shape=jax.ShapeDtypeStruct((m, n), F32),
        compiler_params=_params("parallel", "arbitrary"),
        name="matmul_resid",
    )(a, w, resid)


def _matmul2_resid_kernel(a1_ref, a2_ref, w1_ref, w2_ref, r_ref, o_ref):
    o_ref[...] = r_ref[...] + (_dot(a1_ref[...], w1_ref[...]) + _dot(a2_ref[...], w2_ref[...]))


def matmul2_resid(a1, a2, w, layer, resid, tm, tn):
    m, kh = a1.shape
    n = w.shape[2]
    return pl.pallas_call(
        _matmul2_resid_kernel,
        grid=(m // tm, n // tn),
        in_specs=[
            pl.BlockSpec((tm, kh), lambda i, j: (i, 0)),
            pl.BlockSpec((tm, kh), lambda i, j: (i, 0)),
            pl.BlockSpec((None, kh, tn), lambda i, j: (layer, 0, j)),
            pl.BlockSpec((None, kh, tn), lambda i, j: (layer, 1, j)),
            pl.BlockSpec((tm, tn), lambda i, j: (i, j)),
        ],
        out_specs=pl.BlockSpec((tm, tn), lambda i, j: (i, j)),
        out_shape=jax.ShapeDtypeStruct((m, n), F32),
        compiler_params=_params("parallel", "arbitrary"),
        name="matmul2_resid",
    )(a1, a2, w, w, resid)


def _linear_attention_block(q, k, v, g2, state_ref):
    n = q.shape[0] // CHUNK
    row = lax.broadcasted_iota(jnp.int32, (CHUNK, CHUNK), 0)
    col = lax.broadcasted_iota(jnp.int32, (CHUNK, CHUNK), 1)
    causal = row >= col
    tri = causal.astype(BF16)
    sl = [slice(c * CHUNK, (c + 1) * CHUNK) for c in range(n)]

    g_terms = [_split3(g2[s]) for s in sl]
    b = [_dot(tri, hi) + _dot(tri, mid) + _dot(tri, lo) for hi, mid, lo in g_terms]
    b_mid = [x[CHUNK // 2 - 1:CHUNK // 2] for x in b]
    b_last = [x[CHUNK - 1:] for x in b]
    vb = [v[s].astype(BF16) for s in sl]
    qs = [(q[s] * jnp.exp2(x - m)).astype(BF16) for s, x, m in zip(sl, b, b_mid)]
    ks = [(k[s] * jnp.exp2(m - x)).astype(BF16) for s, x, m in zip(sl, b, b_mid)]
    kd = [(k[s] * jnp.exp2(l - x)).astype(BF16) for s, x, l in zip(sl, b, b_last)]
    qd = [(q[s] * jnp.exp2(x)).astype(BF16) for s, x in zip(sl, b)]

    scores = [jnp.where(causal, _dot_nt(a, c), 0.0).astype(BF16) for a, c in zip(qs, ks)]
    update = [_dot_tn(a, c) for a, c in zip(vb, kd)]
    o_intra = [_dot(a, c) for a, c in zip(scores, vb)]

    state = state_ref[...]
    states = []
    for c in range(n):
        states.append(state.astype(BF16))
        state = state * jnp.exp2(b_last[c]) + update[c]
    state_ref[...] = state

    o = [oi + _dot_nt(a, st) for oi, a, st in zip(o_intra, qd, states)]
    return jnp.concatenate(o, axis=0)


def _hgrn_kernel(*refs, lb_rows, n_riders):
    aq_ref, af_ref, ai_ref, ag_ref, lbl_ref, gain_ref = refs[:6]
    o_ref, state_ref = refs[6 + n_riders], refs[-1]
    _cast_riders(refs[6:6 + n_riders], refs[7 + n_riders:-1])

    @pl.when(pl.program_id(2) == 0)
    def _():
        state_ref[...] = jnp.zeros_like(state_ref)

    logits = lbl_ref[...]
    e = jnp.exp(logits - jnp.max(logits, axis=0, keepdims=True))
    lb = jnp.sum(e[:lb_rows], axis=0, keepdims=True) / jnp.sum(e, axis=0, keepdims=True)

    a_q = aq_ref[0]
    f = lb + (1.0 - lb) * _sigmoid(af_ref[0])
    o = _linear_attention_block(a_q * _sigmoid(a_q), 1.0 - f, ai_ref[0], jnp.log(f) * LOG2_E,
                                state_ref)
    o_ref[0] = (_rms(o, gain_ref[...]) * _sigmoid(ag_ref[0])).astype(o_ref.dtype)


def hgrn_mixer(p, lb_logits, a_idx, gain, ts, riders=()):
    b, s, _ = p.shape
    nh, hd = HGRN_HEADS, HGRN_HEAD_DIM
    nlb = lb_logits.shape[0]
    grid = (b, nh, s // ts)
    r_in, r_out, r_shapes = _rider_specs(riders, grid)

    def col(part):
        return pl.BlockSpec((1, ts, hd), lambda bi, h, si: (bi, si, part * nh + h))

    o, *cast = pl.pallas_call(
        functools.partial(_hgrn_kernel, lb_rows=a_idx + 1, n_riders=len(riders)),
        grid=grid,
        in_specs=[
            col(0), col(1), col(2), col(3),
            pl.BlockSpec((nlb, hd), lambda bi, h, si: (0, h)),
            pl.BlockSpec((1, hd), lambda bi, h, si: (0, 0)),
        ] + r_in,
        out_specs=[pl.BlockSpec((1, ts, hd), lambda bi, h, si: (bi, si, h))] + r_out,
        out_shape=[jax.ShapeDtypeStruct((b, s, nh * hd), BF16)] + r_shapes,
        scratch_shapes=[pltpu.VMEM((hd, hd), F32)],
        compiler_params=_params("parallel", "parallel", "arbitrary"),
        name="hgrn_mixer",
    )(p, p, p, p, lb_logits, gain, *[r.src for r in riders])
    return o, cast


def _gla_kernel(q_ref, k_ref, v_ref, gg_ref, gk_ref, gain_ref, o_ref, state_ref):
    @pl.when(pl.program_id(2) == 0)
    def _():
        state_ref[...] = jnp.zeros_like(state_ref)

    q = q_ref[0] * (GLA_HEAD_K ** -0.5)
    o = _linear_attention_block(q, k_ref[0], v_ref[0], gk_ref[0], state_ref)
    g_g = gg_ref[0]
    o_ref[0] = (_rms(o, gain_ref[...]) * (g_g * _sigmoid(g_g))).astype(o_ref.dtype)


def gla_mixer(p, gk, gain, ts):
    b, s, _ = p.shape
    nh, kd, vd = GLA_HEADS, GLA_HEAD_K, GLA_HEAD_V
    base = 4 * HGRN_HEADS * HGRN_HEAD_DIM
    q0 = base // kd
    k0 = q0 + nh
    v0 = (base + 2 * nh * kd) // vd
    g0 = v0 + nh
    return pl.pallas_call(
        _gla_kernel,
        grid=(b, nh, s // ts),
        in_specs=[
            pl.BlockSpec((1, ts, kd), lambda bi, h, si: (bi, si, q0 + h)),
            pl.BlockSpec((1, ts, kd), lambda bi, h, si: (bi, si, k0 + h)),
            pl.BlockSpec((1, ts, vd), lambda bi, h, si: (bi, si, v0 + h)),
            pl.BlockSpec((1, ts, vd), lambda bi, h, si: (bi, si, g0 + h)),
            pl.BlockSpec((1, ts, kd), lambda bi, h, si: (bi, si, h)),
            pl.BlockSpec((1, vd), lambda bi, h, si: (0, 0)),
        ],
        out_specs=pl.BlockSpec((1, ts, vd), lambda bi, h, si: (bi, si, h)),
        out_shape=jax.ShapeDtypeStruct((b, s, nh * vd), BF16),
        scratch_shapes=[pltpu.VMEM((vd, kd), F32)],
        compiler_params=_params("parallel", "parallel", "arbitrary"),
        name="gla_mixer",
    )(p, p, p, p, gk, gain)


def _sb_kernel(*refs, tq, heads, n_riders):
    q_ref, k_ref, v_ref = refs[:3]
    o_ref = refs[3 + n_riders]
    _cast_riders(refs[3:3 + n_riders], refs[4 + n_riders:])

    qi = pl.program_id(2)
    hd = SB_HEAD_DIM
    row = lax.broadcasted_iota(jnp.int32, (tq, tq), 0)
    col = lax.broadcasted_iota(jnp.int32, (tq, tq), 1)
    behind_ones = (row > col).astype(BF16)
    strictly_causal = col < row
    head_cols = [slice(h * hd, (h + 1) * hd) for h in range(heads)]

    def key_block(j, carry, diagonal):
        start = pl.multiple_of(j * tq, tq)
        acc = [c[0] for c in carry]
        later = [c[1] for c in carry]
        z2 = [_dot_nt(q_ref[0, :, hc], k_ref[0, pl.ds(start, tq), hc]) for hc in head_cols]
        sp2 = [jnp.maximum(z, jnp.log(1.0 + jnp.exp2(jnp.minimum(z, EXP2_CLAMP))) * LOG2_E)
               for z in z2]
        if diagonal:
            sp2 = [jnp.where(strictly_causal, x, 0.0) for x in sp2]
        own = [z - x - lt for z, x, lt in zip(z2, sp2, later)]
        behind = [_dot(x.astype(BF16), behind_ones) for x in sp2]
        log2_a = [o - bh for o, bh in zip(own, behind)]
        if diagonal:
            log2_a = [jnp.where(strictly_causal, x, -jnp.inf) for x in log2_a]
        att = [jnp.exp2(x).astype(BF16) for x in log2_a]
        acc = [a + _dot(p, v_ref[0, pl.ds(start, tq), hc]) for a, p, hc in zip(acc, att, head_cols)]
        later = [lt + (x[:, 0:1] + bh[:, 0:1]) for lt, x, bh in zip(later, sp2, behind)]
        return tuple(zip(acc, later))

    zero = (jnp.zeros((tq, hd), F32), jnp.zeros((tq, 1), F32))
    carry = key_block(qi, (zero,) * heads, True)
    carry = lax.fori_loop(0, qi, lambda t, c: key_block(qi - 1 - t, c, False), carry)
    for h, hc in enumerate(head_cols):
        o_ref[0, :, hc] = carry[h][0].astype(o_ref.dtype)


def sb_attention(qkv, tq, heads, riders=()):
    b, s, _ = qkv.shape
    ng = SB_HEADS // heads
    w = heads * SB_HEAD_DIM
    grid = (b, ng, s // tq)
    r_in, r_out, r_shapes = _rider_specs(riders, grid)
    o, *cast = pl.pallas_call(
        functools.partial(_sb_kernel, tq=tq, heads=heads, n_riders=len(riders)),
        grid=grid,
        in_specs=[
            pl.BlockSpec((1, tq, w), lambda bi, g, qi: (bi, qi, g)),
            pl.BlockSpec((1, s, w), lambda bi, g, qi: (bi, 0, ng + g)),
            pl.BlockSpec((1, s, w), lambda bi, g, qi: (bi, 0, 2 * ng + g)),
        ] + r_in,
        out_specs=[pl.BlockSpec((1, tq, w), lambda bi, g, qi: (bi, qi, g))] + r_out,
        out_shape=[jax.ShapeDtypeStruct((b, s, SB_HEADS * SB_HEAD_DIM), BF16)] + r_shapes,
        compiler_params=_params("parallel", "parallel", "arbitrary"),
        name="sb_attention",
    )(qkv, qkv, qkv, *[r.src for r in riders])
    return o, cast


def _xattn_layer_kernel(x_ref, g_ref, wq_ref, k_ref, v_ref, wo_ref, o_ref):
    x = x_ref[...]
    d = x.shape[-1]
    hd = d // XA_HEADS
    head_cols = [slice(h * hd, (h + 1) * hd) for h in range(XA_HEADS)]
    h_in = [_rms(x[rows], g_ref[...]).astype(BF16)
            for rows in _row_blocks(x.shape[0], min(NORM_SUB_ROWS, x.shape[0]))]
    q = [jnp.concatenate([_dot(hp, wq_ref[:, hc]).astype(BF16) for hp in h_in], axis=0)
         for hc in head_cols]
    scores = [_dot_nt(qh, k_ref[0, :, hc]) * (hd ** -0.5) for qh, hc in zip(q, head_cols)]
    e = [jnp.exp(sc - jnp.max(sc, axis=-1, keepdims=True)) for sc in scores]
    probs = [(eh / jnp.sum(eh, axis=-1, keepdims=True)).astype(BF16) for eh in e]
    o = [_dot(ph, v_ref[0, :, hc]).astype(BF16) for ph, hc in zip(probs, head_cols)]
    o_ref[...] = x + _dot(jnp.concatenate(o, axis=1), wo_ref[...])


def xattn_layer(x, gain, w_q, kv, w_o, layer, seq, tm):
    rows, d = x.shape
    m = kv.shape[1]
    blocks_per_seq = seq // tm
    whole = functools.partial(pl.BlockSpec, (None, d, d), lambda i: (layer, 0, 0),
                              pipeline_mode=pl.Buffered(1))
    return pl.pallas_call(
        _xattn_layer_kernel,
        grid=(rows // tm,),
        in_specs=[
            pl.BlockSpec((tm, d), lambda i: (i, 0)),
            pl.BlockSpec((1, d), lambda i: (0, 0)),
            whole(),
            pl.BlockSpec((1, m, d), lambda i: (i // blocks_per_seq, 0, 0)),
            pl.BlockSpec((1, m, d), lambda i: (i // blocks_per_seq, 0, 1)),
            whole(),
        ],
        out_specs=pl.BlockSpec((tm, d), lambda i: (i, 0)),
        out_shape=jax.ShapeDtypeStruct((rows, d), F32),
        compiler_params=_params("parallel"),
        name="xattn_layer",
    )(x, gain, w_q, kv, kv, w_o)


HALO = BF16_SUBLANES


def _ffn_in_kernel(*refs, subs, blocks_per_seq, n_riders):
    x_ref, halo_ref, g_ref, wu_ref, wg_ref, cw_ref, cb_ref = refs[:7]
    o_ref, h_ref, gs_ref = refs[7 + n_riders], refs[-2], refs[-1]
    _cast_riders(refs[7:7 + n_riders], refs[8 + n_riders:-2])

    inside = pl.program_id(0) % blocks_per_seq != 0
    cw = cw_ref[...]
    cb = cb_ref[...]

    def row_blocks(normalise):
        start = 0
        for sub in subs:
            lo = HALO + start
            first = 0 if start == 0 else lo
            if normalise:
                h_ref[lo:lo + sub, :] = _rms(x_ref[start:start + sub, :], g_ref[...]).astype(BF16)
            gs_ref[first:lo + sub, :] = _dot(h_ref[first:lo + sub, :], wg_ref[...])
            u = _dot(h_ref[lo:lo + sub, :], wu_ref[...])
            conv = 0.0
            for tap in range(CONV_W):
                t0 = lo - (CONV_W - 1) + tap
                conv = conv + cw[tap:tap + 1] * gs_ref[t0:t0 + sub, :]
            gc = cb + conv
            o_ref[start:start + sub, :] = (gc * _sigmoid(gc) * u).astype(o_ref.dtype)
            start += sub

    @pl.when(pl.program_id(1) == 0)
    def _():
        hh = _rms(halo_ref[...], g_ref[...])
        h_ref[:HALO, :] = jnp.where(inside, hh, 0.0).astype(BF16)
        row_blocks(True)

    @pl.when(pl.program_id(1) != 0)
    def _():
        row_blocks(False)


def ffn_in_grid(m, f, tm, tn):
    return (m // tm, f // tn)


def ffn_in(x, gain, w_in, layer, conv_w, conv_b, seq, tm, tn, subs, riders=()):
    m, d = x.shape
    assert sum(subs) == tm
    f = w_in.shape[2] // 2
    grid = ffn_in_grid(m, f, tm, tn)
    nj = grid[1]
    halo_per_block = tm // HALO
    r_in, r_out, r_shapes = _rider_specs(riders, grid)
    o, *cast = pl.pallas_call(
        functools.partial(_ffn_in_kernel, subs=subs, blocks_per_seq=seq // tm,
                          n_riders=len(riders)),
        grid=grid,
        in_specs=[
            pl.BlockSpec((tm, d), lambda i, j: (i, 0)),
            pl.BlockSpec((HALO, d), lambda i, j: (jnp.maximum(i * halo_per_block - 1, 0), 0)),
            pl.BlockSpec((1, d), lambda i, j: (0, 0)),
            pl.BlockSpec((None, d, tn), lambda i, j: (layer, 0, j)),
            pl.BlockSpec((None, d, tn), lambda i, j: (layer, 0, nj + j)),
            pl.BlockSpec((CONV_W, tn), lambda i, j: (0, j)),
            pl.BlockSpec((1, tn), lambda i, j: (0, j)),
        ] + r_in,
        out_specs=[pl.BlockSpec((tm, tn), lambda i, j: (i, j))] + r_out,
        out_shape=[jax.ShapeDtypeStruct((m, f), BF16)] + r_shapes,
        scratch_shapes=[pltpu.VMEM((HALO + tm, d), BF16), pltpu.VMEM((HALO + tm, tn), F32)],
        compiler_params=_params("parallel", "arbitrary"),
        name="ffn_in",
    )(x, x, gain, w_in, w_in, conv_w, conv_b, *[r.src for r in riders])
    return o, cast


def _norm_kernel(x_ref, g_ref, o_ref):
    o_ref[...] = _rms(x_ref[...], g_ref[...])


def rms_norm_rows(x, gain, tm):
    m, d = x.shape
    return pl.pallas_call(
        _norm_kernel,
        grid=(m // tm,),
        in_specs=[pl.BlockSpec((tm, d), lambda i: (i, 0)), pl.BlockSpec((1, d), lambda i: (0, 0))],
        out_specs=pl.BlockSpec((tm, d), lambda i: (i, 0)),
        out_shape=jax.ShapeDtypeStruct((m, d), F32),
        compiler_params=_params("parallel"),
        name="final_norm",
    )(x, gain)


def kernel(x, mem, mem_norm, norm_mix, norm_xattn, norm_ffn, ab_w_in, hgrn_lb_logits, hgrn_norm, gla_w_gk, gla_b_gk, gla_norm, ab_w_out, sb_w_qkv, sb_w_out, xa_w_q, xa_w_kv, xa_w_o, ffn_w_in, ffn_conv_w, ffn_conv_b, ffn_w_out, final_norm):
    b, s, d = x.shape
    depth = norm_mix.shape[0]
    m_len = mem.shape[1]
    rows = b * s
    tm = min(512, s)
    tm_big = min(1024, rows)
    tm_ffn = min(1024, s)
    ffn_subs = (tm_ffn // 2, tm_ffn // 2)
    ts = min(2048, s)
    tq_sb = min(256, s)

    def row(v):
        return v.reshape(1, -1)

    xr = x.reshape(rows, d)

    stacks = dict(ab_w_in=ab_w_in, ab_w_out=ab_w_out, sb_w_qkv=sb_w_qkv, sb_w_out=sb_w_out,
                  xa_w_q=xa_w_q, xa_w_kv=xa_w_kv, xa_w_o=xa_w_o, ffn_w_in=ffn_w_in,
                  ffn_w_out=ffn_w_out)
    ready = {}

    def weight(name, idx):
        if (name, idx) not in ready:
            ready[(name, idx)] = stacks[name][idx:idx + 1].astype(BF16)
        return ready[(name, idx)]

    def plan(wanted, grid):
        keys = [key for key in wanted if key not in ready]
        riders = [_plan_rider(stacks[name], idx, grid) for name, idx in keys]
        keys = [key for key, r in zip(keys, riders) if r is not None]
        return [r for r in riders if r is not None], keys

    def adopt(keys, cast):
        for key, w_bf16 in zip(keys, cast):
            ready[key] = w_bf16[None]

    ab_main = 4 * HGRN_HEADS * HGRN_HEAD_DIM + GLA_HEADS * (2 * GLA_HEAD_K + 2 * GLA_HEAD_V)
    mem_rows = mem.reshape(b * m_len, d)
    d_ff = ffn_w_in.shape[2] // 2

    for layer in range(depth):
        gain = row(norm_mix[layer])
        xattn_weights = [("xa_w_q", layer), ("xa_w_kv", layer), ("xa_w_o", layer)]
        if layer % 2 == 0:
            a = layer // 2
            w_low = jnp.pad(ab_w_in[a][:, ab_main:], ((0, 0), (0, LANES - GLA_GATE_RANK))).astype(BF16)
            w_gk = jnp.pad(gla_w_gk[a], ((0, LANES - GLA_GATE_RANK), (0, 0))).astype(BF16)
            p, gk = ab_proj(xr, gain, weight("ab_w_in", a), 0, ab_main, w_low, w_gk,
                            row(gla_b_gk[a]), tm=tm_big, tn=1024)
            p = p.reshape(b, s, ab_main)
            riders, keys = plan([("ab_w_out", a)] + xattn_weights, (b, HGRN_HEADS, s // ts))
            o_a, cast = hgrn_mixer(p, hgrn_lb_logits, a, row(hgrn_norm[a]), ts, riders)
            adopt(keys, cast)
            o_b = gla_mixer(p, gk.reshape(b, s, -1), row(gla_norm[a]), ts)
            xr = matmul2_resid(o_a.reshape(rows, -1), o_b.reshape(rows, -1),
                               weight("ab_w_out", a), 0, xr, tm=tm, tn=d)
        else:
            c = layer // 2
            qkv = norm_matmul(xr, gain, weight("sb_w_qkv", c), 0, BF16, tm=tm_big, tn=1024,
                              lead_cols=d, lead_scale=(SB_HEAD_DIM ** -0.5) * LOG2_E)
            riders, keys = plan([("sb_w_out", c)] + xattn_weights
                                + [("ffn_w_in", layer), ("ffn_w_out", layer)],
                                (b, SB_HEADS // SB_HEADS_PER_STEP, s // tq_sb))
            o, cast = sb_attention(qkv.reshape(b, s, -1), tq_sb, SB_HEADS_PER_STEP, riders)
            adopt(keys, cast)
            xr = matmul_resid(o.reshape(rows, d), weight("sb_w_out", c), 0, xr, tm=tm, tn=d)

        kv = norm_matmul(mem_rows, row(mem_norm), weight("xa_w_kv", layer), 0, BF16,
                         tm=min(1024, b * m_len), tn=d)
        xr = xattn_layer(xr, row(norm_xattn[layer]), weight("xa_w_q", layer),
                         kv.reshape(b, m_len, 2 * d), weight("xa_w_o", layer), 0, seq=s, tm=tm)

        riders, keys = plan([("ffn_w_out", layer)], ffn_in_grid(rows, d_ff, tm_ffn, 512))
        act, cast = ffn_in(xr, row(norm_ffn[layer]), weight("ffn_w_in", layer), 0,
                           ffn_conv_w[layer], row(ffn_conv_b[layer]), seq=s, tm=tm_ffn, tn=512,
                           subs=ffn_subs, riders=riders)
        adopt(keys, cast)
        xr = matmul_resid(act, weight("ffn_w_out", layer), 0, xr, tm=tm_big, tn=512)

    return rms_norm_rows(xr, row(final_norm), tm).reshape(b, s, d)
```

```python
import functools
from typing import NamedTuple

import jax
import jax.numpy as jnp
from jax import lax
from jax.experimental import pallas as pl
from jax.experimental.pallas import tpu as pltpu

F32 = jnp.float32
BF16 = jnp.bfloat16

RMS_EPS = 1e-6
CHUNK = 64
HGRN_HEADS = 8
HGRN_HEAD_DIM = 128
GLA_HEADS = 4
GLA_HEAD_K = 128
GLA_HEAD_V = 256
GLA_GATE_RANK = 16
GLA_GATE_NORMALIZER = 16.0
SB_HEADS = 16
SB_HEAD_DIM = 128
XA_HEADS = 4
CONV_W = 3
LOG2_E = 1.4426950408889634
EXP2_CLAMP = 126.0
SB_HEADS_PER_STEP = 4

LANES = 128
BF16_SUBLANES = 16
VMEM_LIMIT_BYTES = 52 * 1024 * 1024
NORM_SUB_ROWS = 256


def _params(*semantics):
    return pltpu.CompilerParams(dimension_semantics=semantics, vmem_limit_bytes=VMEM_LIMIT_BYTES)


def _rms(x, gain):
    return x * lax.rsqrt(jnp.mean(x * x, axis=-1, keepdims=True) + RMS_EPS) * gain


def _sigmoid(x):
    return 1.0 / (1.0 + jnp.exp(-x))


def _softplus(x):
    return jnp.maximum(x, 0.0) + jnp.log(1.0 + jnp.exp(-jnp.abs(x)))


def _dot(a, b):
    return jnp.dot(a, b, preferred_element_type=F32)


def _dot_nt(a, b):
    return lax.dot_general(a, b, (((1,), (1,)), ((), ())), preferred_element_type=F32)


def _dot_tn(a, b):
    return lax.dot_general(a, b, (((0,), (0,)), ((), ())), preferred_element_type=F32)


def _split3(x):
    hi = x.astype(BF16)
    r1 = x - hi.astype(F32)
    mid = r1.astype(BF16)
    lo = (r1 - mid.astype(F32)).astype(BF16)
    return hi, mid, lo


class _Rider(NamedTuple):
    src: jax.Array
    layer: int
    row_blocks: int
    col_blocks: int


def _plan_rider(src, layer, grid):
    steps = 1
    for g in grid:
        steps *= g
    k, n = src.shape[-2:]
    for col_blocks in range(1, steps + 1):
        if steps % col_blocks or n % (col_blocks * LANES):
            continue
        row_blocks = steps // col_blocks
        if k % (row_blocks * BF16_SUBLANES) == 0:
            return _Rider(src, layer, row_blocks, col_blocks)
    return None


def _rider_specs(riders, grid):
    def step(*ids):
        lin = ids[0]
        for g, i in zip(grid[1:], ids[1:]):
            lin = lin * g + i
        return lin

    in_specs, out_specs, out_shapes = [], [], []
    for r in riders:
        k, n = r.src.shape[-2:]
        block = (k // r.row_blocks, n // r.col_blocks)

        def block_index(*ids, cb=r.col_blocks):
            lin = step(*ids)
            return lin // cb, lin % cb

        def src_index(*ids, layer=r.layer, f=block_index):
            return (layer,) + tuple(f(*ids))

        in_specs.append(pl.BlockSpec((None,) + block, src_index))
        out_specs.append(pl.BlockSpec(block, block_index))
        out_shapes.append(jax.ShapeDtypeStruct((k, n), BF16))
    return in_specs, out_specs, out_shapes


def _cast_riders(src_refs, dst_refs):
    for src_ref, dst_ref in zip(src_refs, dst_refs):
        dst_ref[...] = src_ref[...].astype(BF16)


def _row_blocks(total, size):
    return [slice(r, r + size) for r in range(0, total, size)]


def _norm_matmul_kernel(x_ref, g_ref, w_ref, o_ref, h_ref, *, lead_blocks, lead_scale, sub):
    j = pl.program_id(1)

    def project(h):
        acc = _dot(h, w_ref[...])
        if lead_blocks:
            acc = acc * jnp.where(j < lead_blocks, lead_scale, 1.0)
        return acc.astype(o_ref.dtype)

    @pl.when(j == 0)
    def _():
        for rows in _row_blocks(x_ref.shape[0], sub):
            h = _rms(x_ref[rows, :], g_ref[...]).astype(BF16)
            h_ref[rows, :] = h
            o_ref[rows, :] = project(h)

    @pl.when(j != 0)
    def _():
        o_ref[...] = project(h_ref[...])


def norm_matmul(x, gain, w, layer, out_dtype, tm, tn, lead_cols=0, lead_scale=1.0):
    m, k = x.shape
    n = w.shape[2]
    assert lead_cols % tn == 0
    return pl.pallas_call(
        functools.partial(_norm_matmul_kernel, lead_blocks=lead_cols // tn, lead_scale=lead_scale,
                          sub=min(NORM_SUB_ROWS, tm)),
        grid=(m // tm, n // tn),
        in_specs=[
            pl.BlockSpec((tm, k), lambda i, j: (i, 0)),
            pl.BlockSpec((1, k), lambda i, j: (0, 0)),
            pl.BlockSpec((None, k, tn), lambda i, j: (layer, 0, j)),
        ],
        out_specs=pl.BlockSpec((tm, tn), lambda i, j: (i, j)),
        out_shape=jax.ShapeDtypeStruct((m, n), out_dtype),
        scratch_shapes=[pltpu.VMEM((tm, k), BF16)],
        compiler_params=_params("parallel", "arbitrary"),
        name="norm_matmul",
    )(x, gain, w)


def _ab_proj_kernel(x_ref, g_ref, w_ref, wlow_ref, wgk_ref, bgk_ref, o_ref, gk_ref, h_ref, *, sub):
    @pl.when(pl.program_id(1) == 0)
    def _():
        for rows in _row_blocks(x_ref.shape[0], sub):
            h = _rms(x_ref[rows, :], g_ref[...]).astype(BF16)
            h_ref[rows, :] = h
            o_ref[rows, :] = _dot(h, w_ref[...])
            low = _dot(h, wlow_ref[...])
            pre = _dot(low.astype(BF16), wgk_ref[...]) + bgk_ref[...]
            gk_ref[rows, :] = -_softplus(-pre) * (LOG2_E / GLA_GATE_NORMALIZER)

    @pl.when(pl.program_id(1) != 0)
    def _():
        o_ref[...] = _dot(h_ref[...], w_ref[...])


def ab_proj(x, gain, w, layer, n, w_low, w_gk, b_gk, tm, tn):
    m, k = x.shape
    assert n % tn == 0
    ngk = w_gk.shape[1]
    return pl.pallas_call(
        functools.partial(_ab_proj_kernel, sub=min(NORM_SUB_ROWS, tm)),
        grid=(m // tm, n // tn),
        in_specs=[
            pl.BlockSpec((tm, k), lambda i, j: (i, 0)),
            pl.BlockSpec((1, k), lambda i, j: (0, 0)),
            pl.BlockSpec((None, k, tn), lambda i, j: (layer, 0, j)),
            pl.BlockSpec((k, LANES), lambda i, j: (0, 0)),
            pl.BlockSpec((LANES, ngk), lambda i, j: (0, 0)),
            pl.BlockSpec((1, ngk), lambda i, j: (0, 0)),
        ],
        out_specs=[
            pl.BlockSpec((tm, tn), lambda i, j: (i, j)),
            pl.BlockSpec((tm, ngk), lambda i, j: (i, 0)),
        ],
        out_shape=[
            jax.ShapeDtypeStruct((m, n), F32),
            jax.ShapeDtypeStruct((m, ngk), F32),
        ],
        scratch_shapes=[pltpu.VMEM((tm, k), BF16)],
        compiler_params=_params("parallel", "arbitrary"),
        name="ab_proj",
    )(x, gain, w, w_low, w_gk, b_gk)


def _matmul_resid_kernel(a_ref, w_ref, r_ref, o_ref):
    o_ref[...] = r_ref[...] + _dot(a_ref[...], w_ref[...])


def matmul_resid(a, w, layer, resid, tm, tn):
    m, k = a.shape
    n = w.shape[2]
    return pl.pallas_call(
        _matmul_resid_kernel,
        grid=(m // tm, n // tn),
        in_specs=[
            pl.BlockSpec((tm, k), lambda i, j: (i, 0)),
            pl.BlockSpec((None, k, tn), lambda i, j: (layer, 0, j)),
            pl.BlockSpec((tm, tn), lambda i, j: (i, j)),
        ],
        out_specs=pl.BlockSpec((tm, tn), lambda i, j: (i, j)),
        out_shape=jax.ShapeDtypeStruct((m, n), F32),
        compiler_params=_params("parallel", "arbitrary"),
        name="matmul_resid",
    )(a, w, resid)


def _matmul_resid_norm_kernel(a_ref, w_ref, r_ref, g_ref, o_ref, *, sub):
    for rows in _row_blocks(a_ref.shape[0], sub):
        y = r_ref[rows, :] + _dot(a_ref[rows, :], w_ref[...])
        o_ref[rows, :] = _rms(y, g_ref[...])


def matmul_resid_norm(a, w, layer, resid, gain, tm):
    m, k = a.shape
    n = w.shape[2]
    return pl.pallas_call(
        functools.partial(_matmul_resid_norm_kernel, sub=tm // 2),
        grid=(m // tm,),
        in_specs=[
            pl.BlockSpec((tm, k), lambda i: (i, 0)),
            pl.BlockSpec((None, k, n), lambda i: (layer, 0, 0), pipeline_mode=pl.Buffered(1)),
            pl.BlockSpec((tm, n), lambda i: (i, 0)),
            pl.BlockSpec((1, n), lambda i: (0, 0)),
        ],
        out_specs=pl.BlockSpec((tm, n), lambda i: (i, 0)),
        out_shape=jax.ShapeDtypeStruct((m, n), F32),
        compiler_params=_params("parallel"),
        name="matmul_resid_norm",
    )(a, w, resid, gain)


def _matmul2_resid_kernel(a1_ref, a2_ref, w1_ref, w2_ref, r_ref, o_ref):
    o_ref[...] = r_ref[...] + (_dot(a1_ref[...], w1_ref[...]) + _dot(a2_ref[...], w2_ref[...]))


def matmul2_resid(a1, a2, w, layer, resid, tm, tn):
    m, kh = a1.shape
    n = w.shape[2]
    return pl.pallas_call(
        _matmul2_resid_kernel,
        grid=(m // tm, n // tn),
        in_specs=[
            pl.BlockSpec((tm, kh), lambda i, j: (i, 0)),
            pl.BlockSpec((tm, kh), lambda i, j: (i, 0)),
            pl.BlockSpec((None, kh, tn), lambda i, j: (layer, 0, j)),
            pl.BlockSpec((None, kh, tn), lambda i, j: (layer, 1, j)),
            pl.BlockSpec((tm, tn), lambda i, j: (i, j)),
        ],
        out_specs=pl.BlockSpec((tm, tn), lambda i, j: (i, j)),
        out_shape=jax.ShapeDtypeStruct((m, n), F32),
        compiler_params=_params("parallel", "arbitrary"),
        name="matmul2_resid",
    )(a1, a2, w, w, resid)


def _linear_attention_block(q, k, v, g2, state_ref):
    n = q.shape[0] // CHUNK
    row = lax.broadcasted_iota(jnp.int32, (CHUNK, CHUNK), 0)
    col = lax.broadcasted_iota(jnp.int32, (CHUNK, CHUNK), 1)
    causal = row >= col
    tri = causal.astype(BF16)
    sl = [slice(c * CHUNK, (c + 1) * CHUNK) for c in range(n)]

    g_terms = [_split3(g2[s]) for s in sl]
    b = [_dot(tri, hi) + _dot(tri, mid) + _dot(tri, lo) for hi, mid, lo in g_terms]
    b_mid = [x[CHUNK // 2 - 1:CHUNK // 2] for x in b]
    b_last = [x[CHUNK - 1:] for x in b]
    vb = [v[s].astype(BF16) for s in sl]
    qs = [(q[s] * jnp.exp2(x - m)).astype(BF16) for s, x, m in zip(sl, b, b_mid)]
    ks = [(k[s] * jnp.exp2(m - x)).astype(BF16) for s, x, m in zip(sl, b, b_mid)]
    kd = [(k[s] * jnp.exp2(l - x)).astype(BF16) for s, x, l in zip(sl, b, b_last)]
    qd = [(q[s] * jnp.exp2(x)).astype(BF16) for s, x in zip(sl, b)]

    scores = [jnp.where(causal, _dot_nt(a, c), 0.0).astype(BF16) for a, c in zip(qs, ks)]
    update = [_dot_tn(a, c) for a, c in zip(vb, kd)]
    o_intra = [_dot(a, c) for a, c in zip(scores, vb)]

    state = state_ref[...]
    states = []
    for c in range(n):
        states.append(state.astype(BF16))
        state = state * jnp.exp2(b_last[c]) + update[c]
    state_ref[...] = state

    o = [oi + _dot_nt(a, st) for oi, a, st in zip(o_intra, qd, states)]
    return jnp.concatenate(o, axis=0)


def _hgrn_kernel(*refs, lb_rows, n_riders):
    aq_ref, af_ref, ai_ref, ag_ref, lbl_ref, gain_ref = refs[:6]
    o_ref, state_ref = refs[6 + n_riders], refs[-1]
    _cast_riders(refs[6:6 + n_riders], refs[7 + n_riders:-1])

    @pl.when(pl.program_id(2) == 0)
    def _():
        state_ref[...] = jnp.zeros_like(state_ref)

    logits = lbl_ref[...]
    e = jnp.exp(logits - jnp.max(logits, axis=0, keepdims=True))
    lb = jnp.sum(e[:lb_rows], axis=0, keepdims=True) / jnp.sum(e, axis=0, keepdims=True)

    a_q = aq_ref[0]
    f = lb + (1.0 - lb) * _sigmoid(af_ref[0])
    o = _linear_attention_block(a_q * _sigmoid(a_q), 1.0 - f, ai_ref[0], jnp.log(f) * LOG2_E,
                                state_ref)
    o_ref[0] = (_rms(o, gain_ref[...]) * _sigmoid(ag_ref[0])).astype(o_ref.dtype)


def hgrn_mixer(p, lb_logits, a_idx, gain, ts, riders=()):
    b, s, _ = p.shape
    nh, hd = HGRN_HEADS, HGRN_HEAD_DIM
    nlb = lb_logits.shape[0]
    grid = (b, nh, s // ts)
    r_in, r_out, r_shapes = _rider_specs(riders, grid)

    def col(part):
        return pl.BlockSpec((1, ts, hd), lambda bi, h, si: (bi, si, part * nh + h))

    o, *cast = pl.pallas_call(
        functools.partial(_hgrn_kernel, lb_rows=a_idx + 1, n_riders=len(riders)),
        grid=grid,
        in_specs=[
            col(0), col(1), col(2), col(3),
            pl.BlockSpec((nlb, hd), lambda bi, h, si: (0, h)),
            pl.BlockSpec((1, hd), lambda bi, h, si: (0, 0)),
        ] + r_in,
        out_specs=[pl.BlockSpec((1, ts, hd), lambda bi, h, si: (bi, si, h))] + r_out,
        out_shape=[jax.ShapeDtypeStruct((b, s, nh * hd), BF16)] + r_shapes,
        scratch_shapes=[pltpu.VMEM((hd, hd), F32)],
        compiler_params=_params("parallel", "parallel", "arbitrary"),
        name="hgrn_mixer",
    )(p, p, p, p, lb_logits, gain, *[r.src for r in riders])
    return o, cast


def _gla_kernel(q_ref, k_ref, v_ref, gg_ref, gk_ref, gain_ref, o_ref, state_ref):
    @pl.when(pl.program_id(2) == 0)
    def _():
        state_ref[...] = jnp.zeros_like(state_ref)

    q = q_ref[0] * (GLA_HEAD_K ** -0.5)
    o = _linear_attention_block(q, k_ref[0], v_ref[0], gk_ref[0], state_ref)
    g_g = gg_ref[0]
    o_ref[0] = (_rms(o, gain_ref[...]) * (g_g * _sigmoid(g_g))).astype(o_ref.dtype)


def gla_mixer(p, gk, gain, ts):
    b, s, _ = p.shape
    nh, kd, vd = GLA_HEADS, GLA_HEAD_K, GLA_HEAD_V
    base = 4 * HGRN_HEADS * HGRN_HEAD_DIM
    q0 = base // kd
    k0 = q0 + nh
    v0 = (base + 2 * nh * kd) // vd
    g0 = v0 + nh
    return pl.pallas_call(
        _gla_kernel,
        grid=(b, nh, s // ts),
        in_specs=[
            pl.BlockSpec((1, ts, kd), lambda bi, h, si: (bi, si, q0 + h)),
            pl.BlockSpec((1, ts, kd), lambda bi, h, si: (bi, si, k0 + h)),
            pl.BlockSpec((1, ts, vd), lambda bi, h, si: (bi, si, v0 + h)),
            pl.BlockSpec((1, ts, vd), lambda bi, h, si: (bi, si, g0 + h)),
            pl.BlockSpec((1, ts, kd), lambda bi, h, si: (bi, si, h)),
            pl.BlockSpec((1, vd), lambda bi, h, si: (0, 0)),
        ],
        out_specs=pl.BlockSpec((1, ts, vd), lambda bi, h, si: (bi, si, h)),
        out_shape=jax.ShapeDtypeStruct((b, s, nh * vd), BF16),
        scratch_shapes=[pltpu.VMEM((vd, kd), F32)],
        compiler_params=_params("parallel", "parallel", "arbitrary"),
        name="gla_mixer",
    )(p, p, p, p, gk, gain)


def _sb_kernel(*refs, tq, heads, n_riders):
    q_ref, k_ref, v_ref = refs[:3]
    o_ref = refs[3 + n_riders]
    _cast_riders(refs[3:3 + n_riders], refs[4 + n_riders:])

    qi = pl.program_id(2)
    hd = SB_HEAD_DIM
    row = lax.broadcasted_iota(jnp.int32, (tq, tq), 0)
    col = lax.broadcasted_iota(jnp.int32, (tq, tq), 1)
    behind_ones = (row > col).astype(BF16)
    strictly_causal = col < row
    head_cols = [slice(h * hd, (h + 1) * hd) for h in range(heads)]

    def key_block(j, carry, diagonal):
        start = pl.multiple_of(j * tq, tq)
        acc = [c[0] for c in carry]
        later = [c[1] for c in carry]
        z2 = [_dot_nt(q_ref[0, :, hc], k_ref[0, pl.ds(start, tq), hc]) for hc in head_cols]
        sp2 = [jnp.maximum(z, jnp.log(1.0 + jnp.exp2(jnp.minimum(z, EXP2_CLAMP))) * LOG2_E)
               for z in z2]
        if diagonal:
            sp2 = [jnp.where(strictly_causal, x, 0.0) for x in sp2]
        own = [z - x - lt for z, x, lt in zip(z2, sp2, later)]
        behind = [_dot(x.astype(BF16), behind_ones) for x in sp2]
        log2_a = [o - bh for o, bh in zip(own, behind)]
        if diagonal:
            log2_a = [jnp.where(strictly_causal, x, -jnp.inf) for x in log2_a]
        att = [jnp.exp2(x).astype(BF16) for x in log2_a]
        acc = [a + _dot(p, v_ref[0, pl.ds(start, tq), hc]) for a, p, hc in zip(acc, att, head_cols)]
        later = [lt + (x[:, 0:1] + bh[:, 0:1]) for lt, x, bh in zip(later, sp2, behind)]
        return tuple(zip(acc, later))

    zero = (jnp.zeros((tq, hd), F32), jnp.zeros((tq, 1), F32))
    carry = key_block(qi, (zero,) * heads, True)
    carry = lax.fori_loop(0, qi, lambda t, c: key_block(qi - 1 - t, c, False), carry)
    for h, hc in enumerate(head_cols):
        o_ref[0, :, hc] = carry[h][0].astype(o_ref.dtype)


def sb_attention(qkv, tq, heads, riders=()):
    b, s, _ = qkv.shape
    ng = SB_HEADS // heads
    w = heads * SB_HEAD_DIM
    grid = (b, ng, s // tq)
    r_in, r_out, r_shapes = _rider_specs(riders, grid)
    o, *cast = pl.pallas_call(
        functools.partial(_sb_kernel, tq=tq, heads=heads, n_riders=len(riders)),
        grid=grid,
        in_specs=[
            pl.BlockSpec((1, tq, w), lambda bi, g, qi: (bi, qi, g)),
            pl.BlockSpec((1, s, w), lambda bi, g, qi: (bi, 0, ng + g)),
            pl.BlockSpec((1, s, w), lambda bi, g, qi: (bi, 0, 2 * ng + g)),
        ] + r_in,
        out_specs=[pl.BlockSpec((1, tq, w), lambda bi, g, qi: (bi, qi, g))] + r_out,
        out_shape=[jax.ShapeDtypeStruct((b, s, SB_HEADS * SB_HEAD_DIM), BF16)] + r_shapes,
        compiler_params=_params("parallel", "parallel", "arbitrary"),
        name="sb_attention",
    )(qkv, qkv, qkv, *[r.src for r in riders])
    return o, cast


def _xattn_layer_kernel(x_ref, g_ref, wq_ref, k_ref, v_ref, wo_ref, o_ref):
    x = x_ref[...]
    d = x.shape[-1]
    hd = d // XA_HEADS
    head_cols = [slice(h * hd, (h + 1) * hd) for h in range(XA_HEADS)]
    h_in = [_rms(x[rows], g_ref[...]).astype(BF16)
            for rows in _row_blocks(x.shape[0], min(NORM_SUB_ROWS, x.shape[0]))]
    q = [jnp.concatenate([_dot(hp, wq_ref[:, hc]).astype(BF16) for hp in h_in], axis=0)
         for hc in head_cols]
    scores = [_dot_nt(qh, k_ref[0, :, hc]) * (hd ** -0.5) for qh, hc in zip(q, head_cols)]
    e = [jnp.exp(sc - jnp.max(sc, axis=-1, keepdims=True)) for sc in scores]
    probs = [(eh / jnp.sum(eh, axis=-1, keepdims=True)).astype(BF16) for eh in e]
    o = [_dot(ph, v_ref[0, :, hc]).astype(BF16) for ph, hc in zip(probs, head_cols)]
    o_ref[...] = x + _dot(jnp.concatenate(o, axis=1), wo_ref[...])


def xattn_layer(x, gain, w_q, kv, w_o, layer, seq, tm):
    rows, d = x.shape
    m = kv.shape[1]
    blocks_per_seq = seq // tm
    whole = functools.partial(pl.BlockSpec, (None, d, d), lambda i: (layer, 0, 0),
                              pipeline_mode=pl.Buffered(1))
    return pl.pallas_call(
        _xattn_layer_kernel,
        grid=(rows // tm,),
        in_specs=[
            pl.BlockSpec((tm, d), lambda i: (i, 0)),
            pl.BlockSpec((1, d), lambda i: (0, 0)),
            whole(),
            pl.BlockSpec((1, m, d), lambda i: (i // blocks_per_seq, 0, 0)),
            pl.BlockSpec((1, m, d), lambda i: (i // blocks_per_seq, 0, 1)),
            whole(),
        ],
        out_specs=pl.BlockSpec((tm, d), lambda i: (i, 0)),
        out_shape=jax.ShapeDtypeStruct((rows, d), F32),
        compiler_params=_params("parallel"),
        name="xattn_layer",
    )(x, gain, w_q, kv, kv, w_o)


HALO = BF16_SUBLANES


def _ffn_in_kernel(*refs, subs, blocks_per_seq, n_riders):
    x_ref, halo_ref, g_ref, wu_ref, wg_ref, cw_ref, cb_ref = refs[:7]
    o_ref, h_ref, gs_ref = refs[7 + n_riders], refs[-2], refs[-1]
    _cast_riders(refs[7:7 + n_riders], refs[8 + n_riders:-2])

    inside = pl.program_id(0) % blocks_per_seq != 0
    cw = cw_ref[...]
    cb = cb_ref[...]

    def row_blocks(normalise):
        start = 0
        for sub in subs:
            lo = HALO + start
            first = 0 if start == 0 else lo
            if normalise:
                h_ref[lo:lo + sub, :] = _rms(x_ref[start:start + sub, :], g_ref[...]).astype(BF16)
            gs_ref[first:lo + sub, :] = _dot(h_ref[first:lo + sub, :], wg_ref[...])
            u = _dot(h_ref[lo:lo + sub, :], wu_ref[...])
            conv = 0.0
            for tap in range(CONV_W):
                t0 = lo - (CONV_W - 1) + tap
                conv = conv + cw[tap:tap + 1] * gs_ref[t0:t0 + sub, :]
            gc = cb + conv
            o_ref[start:start + sub, :] = (gc * _sigmoid(gc) * u).astype(o_ref.dtype)
            start += sub

    @pl.when(pl.program_id(1) == 0)
    def _():
        hh = _rms(halo_ref[...], g_ref[...])
        h_ref[:HALO, :] = jnp.where(inside, hh, 0.0).astype(BF16)
        row_blocks(True)

    @pl.when(pl.program_id(1) != 0)
    def _():
        row_blocks(False)


def ffn_in_grid(m, f, tm, tn):
    return (m // tm, f // tn)


def ffn_in(x, gain, w_in, layer, conv_w, conv_b, seq, tm, tn, subs, riders=()):
    m, d = x.shape
    assert sum(subs) == tm
    f = w_in.shape[2] // 2
    grid = ffn_in_grid(m, f, tm, tn)
    nj = grid[1]
    halo_per_block = tm // HALO
    r_in, r_out, r_shapes = _rider_specs(riders, grid)
    o, *cast = pl.pallas_call(
        functools.partial(_ffn_in_kernel, subs=subs, blocks_per_seq=seq // tm,
                          n_riders=len(riders)),
        grid=grid,
        in_specs=[
            pl.BlockSpec((tm, d), lambda i, j: (i, 0)),
            pl.BlockSpec((HALO, d), lambda i, j: (jnp.maximum(i * halo_per_block - 1, 0), 0)),
            pl.BlockSpec((1, d), lambda i, j: (0, 0)),
            pl.BlockSpec((None, d, tn), lambda i, j: (layer, 0, j)),
            pl.BlockSpec((None, d, tn), lambda i, j: (layer, 0, nj + j)),
            pl.BlockSpec((CONV_W, tn), lambda i, j: (0, j)),
            pl.BlockSpec((1, tn), lambda i, j: (0, j)),
        ] + r_in,
        out_specs=[pl.BlockSpec((tm, tn), lambda i, j: (i, j))] + r_out,
        out_shape=[jax.ShapeDtypeStruct((m, f), BF16)] + r_shapes,
        scratch_shapes=[pltpu.VMEM((HALO + tm, d), BF16), pltpu.VMEM((HALO + tm, tn), F32)],
        compiler_params=_params("parallel", "arbitrary"),
        name="ffn_in",
    )(x, x, gain, w_in, w_in, conv_w, conv_b, *[r.src for r in riders])
    return o, cast


def kernel(x, mem, mem_norm, norm_mix, norm_xattn, norm_ffn, ab_w_in, hgrn_lb_logits, hgrn_norm, gla_w_gk, gla_b_gk, gla_norm, ab_w_out, sb_w_qkv, sb_w_out, xa_w_q, xa_w_kv, xa_w_o, ffn_w_in, ffn_conv_w, ffn_conv_b, ffn_w_out, final_norm):
    b, s, d = x.shape
    depth = norm_mix.shape[0]
    m_len = mem.shape[1]
    rows = b * s
    tm = min(512, s)
    tm_big = min(1024, rows)
    tm_ffn = min(1024, s)
    ffn_subs = (tm_ffn // 2, tm_ffn // 2)
    ts = min(2048, s)
    tq_sb = min(256, s)

    def row(v):
        return v.reshape(1, -1)

    xr = x.reshape(rows, d)

    stacks = dict(ab_w_in=ab_w_in, ab_w_out=ab_w_out, sb_w_qkv=sb_w_qkv, sb_w_out=sb_w_out,
                  xa_w_q=xa_w_q, xa_w_kv=xa_w_kv, xa_w_o=xa_w_o, ffn_w_in=ffn_w_in,
                  ffn_w_out=ffn_w_out)
    ready = {}

    def weight(name, idx):
        if (name, idx) not in ready:
            ready[(name, idx)] = stacks[name][idx:idx + 1].astype(BF16)
        return ready[(name, idx)]

    def plan(wanted, grid):
        keys = [key for key in wanted if key not in ready]
        riders = [_plan_rider(stacks[name], idx, grid) for name, idx in keys]
        keys = [key for key, r in zip(keys, riders) if r is not None]
        return [r for r in riders if r is not None], keys

    def adopt(keys, cast):
        for key, w_bf16 in zip(keys, cast):
            ready[key] = w_bf16[None]

    ab_main = 4 * HGRN_HEADS * HGRN_HEAD_DIM + GLA_HEADS * (2 * GLA_HEAD_K + 2 * GLA_HEAD_V)
    mem_rows = mem.reshape(b * m_len, d)
    d_ff = ffn_w_in.shape[2] // 2

    for layer in range(depth):
        gain = row(norm_mix[layer])
        xattn_weights = [("xa_w_q", layer), ("xa_w_kv", layer), ("xa_w_o", layer)]
        if layer % 2 == 0:
            a = layer // 2
            w_low = jnp.pad(ab_w_in[a][:, ab_main:], ((0, 0), (0, LANES - GLA_GATE_RANK))).astype(BF16)
            w_gk = jnp.pad(gla_w_gk[a], ((0, LANES - GLA_GATE_RANK), (0, 0))).astype(BF16)
            p, gk = ab_proj(xr, gain, weight("ab_w_in", a), 0, ab_main, w_low, w_gk,
                            row(gla_b_gk[a]), tm=tm_big, tn=1024)
            p = p.reshape(b, s, ab_main)
            riders, keys = plan([("ab_w_out", a)] + xattn_weights, (b, HGRN_HEADS, s // ts))
            o_a, cast = hgrn_mixer(p, hgrn_lb_logits, a, row(hgrn_norm[a]), ts, riders)
            adopt(keys, cast)
            o_b = gla_mixer(p, gk.reshape(b, s, -1), row(gla_norm[a]), ts)
            xr = matmul2_resid(o_a.reshape(rows, -1), o_b.reshape(rows, -1),
                               weight("ab_w_out", a), 0, xr, tm=tm, tn=d)
        else:
            c = layer // 2
            qkv = norm_matmul(xr, gain, weight("sb_w_qkv", c), 0, BF16, tm=tm_big, tn=1024,
                              lead_cols=d, lead_scale=(SB_HEAD_DIM ** -0.5) * LOG2_E)
            riders, keys = plan([("sb_w_out", c)] + xattn_weights
                                + [("ffn_w_in", layer), ("ffn_w_out", layer)],
                                (b, SB_HEADS // SB_HEADS_PER_STEP, s // tq_sb))
            o, cast = sb_attention(qkv.reshape(b, s, -1), tq_sb, SB_HEADS_PER_STEP, riders)
            adopt(keys, cast)
            xr = matmul_resid(o.reshape(rows, d), weight("sb_w_out", c), 0, xr, tm=tm, tn=d)

        kv = norm_matmul(mem_rows, row(mem_norm), weight("xa_w_kv", layer), 0, BF16,
                         tm=min(1024, b * m_len), tn=d)
        xr = xattn_layer(xr, row(norm_xattn[layer]), weight("xa_w_q", layer),
                         kv.reshape(b, m_len, 2 * d), weight("xa_w_o", layer), 0, seq=s, tm=tm)

        riders, keys = plan([("ffn_w_out", layer)], ffn_in_grid(rows, d_ff, tm_ffn, 512))
        act, cast = ffn_in(xr, row(norm_ffn[layer]), weight("ffn_w_in", layer), 0,
                           ffn_conv_w[layer], row(ffn_conv_b[layer]), seq=s, tm=tm_ffn, tn=512,
                           subs=ffn_subs, riders=riders)
        adopt(keys, cast)
        if layer < depth - 1:
            xr = matmul_resid(act, weight("ffn_w_out", layer), 0, xr, tm=tm_big, tn=512)
        else:
            xr = matmul_resid_norm(act, weight("ffn_w_out", layer), 0, xr, row(final_norm),
                                   tm=min(256, rows))

    return xr.reshape(b, s, d)
```

```python
import functools
from typing import NamedTuple

import jax
import jax.numpy as jnp
from jax import lax
from jax.experimental import pallas as pl
from jax.experimental.pallas import tpu as pltpu

F32 = jnp.float32
BF16 = jnp.bfloat16

RMS_EPS = 1e-6
CHUNK = 64
HGRN_HEADS = 8
HGRN_HEAD_DIM = 128
GLA_HEADS = 4
GLA_HEAD_K = 128
GLA_HEAD_V = 256
GLA_GATE_RANK = 16
GLA_GATE_NORMALIZER = 16.0
SB_HEADS = 16
SB_HEAD_DIM = 128
XA_HEADS = 4
CONV_W = 3
LOG2_E = 1.4426950408889634
EXP2_CLAMP = 126.0
SB_HEADS_PER_STEP = 4

LANES = 128
BF16_SUBLANES = 16
VMEM_LIMIT_BYTES = 52 * 1024 * 1024
NORM_SUB_ROWS = 256


def _params(*semantics):
    return pltpu.CompilerParams(dimension_semantics=semantics, vmem_limit_bytes=VMEM_LIMIT_BYTES)


def _rms(x, gain):
    return x * lax.rsqrt(jnp.mean(x * x, axis=-1, keepdims=True) + RMS_EPS) * gain


def _sigmoid(x):
    return 1.0 / (1.0 + jnp.exp(-x))


def _softplus(x):
    return jnp.maximum(x, 0.0) + jnp.log(1.0 + jnp.exp(-jnp.abs(x)))


def _dot(a, b):
    return jnp.dot(a, b, preferred_element_type=F32)


def _dot_nt(a, b):
    return lax.dot_general(a, b, (((1,), (1,)), ((), ())), preferred_element_type=F32)


def _dot_tn(a, b):
    return lax.dot_general(a, b, (((0,), (0,)), ((), ())), preferred_element_type=F32)


def _split3(x):
    hi = x.astype(BF16)
    r1 = x - hi.astype(F32)
    mid = r1.astype(BF16)
    lo = (r1 - mid.astype(F32)).astype(BF16)
    return hi, mid, lo


class _Rider(NamedTuple):
    src: jax.Array
    layer: int
    row_blocks: int
    col_blocks: int
    axes: int


def _plan_rider(src, layer, grid, axes=None):
    axes = len(grid) if axes is None else axes
    steps = 1
    for g in grid[:axes]:
        steps *= g
    k, n = src.shape[-2:]
    for col_blocks in range(1, steps + 1):
        if steps % col_blocks or n % (col_blocks * LANES):
            continue
        row_blocks = steps // col_blocks
        if k % (row_blocks * BF16_SUBLANES) == 0:
            return _Rider(src, layer, row_blocks, col_blocks, axes)
    return None


def _rider_specs(riders, grid):
    def step(ids):
        lin = ids[0]
        for g, i in zip(grid[1:len(ids)], ids[1:]):
            lin = lin * g + i
        return lin

    in_specs, out_specs, out_shapes = [], [], []
    for r in riders:
        k, n = r.src.shape[-2:]
        block = (k // r.row_blocks, n // r.col_blocks)

        def block_index(*ids, cb=r.col_blocks, axes=r.axes):
            lin = step(ids[:axes])
            return lin // cb, lin % cb

        def src_index(*ids, layer=r.layer, f=block_index):
            return (layer,) + tuple(f(*ids))

        in_specs.append(pl.BlockSpec((None,) + block, src_index))
        out_specs.append(pl.BlockSpec(block, block_index))
        out_shapes.append(jax.ShapeDtypeStruct((k, n), BF16))
    return in_specs, out_specs, out_shapes


def _cast_riders(src_refs, dst_refs, inner_axes=()):
    for n, (src_ref, dst_ref) in enumerate(zip(src_refs, dst_refs)):
        def cast(src_ref=src_ref, dst_ref=dst_ref):
            dst_ref[...] = src_ref[...].astype(BF16)

        held = inner_axes[n] if inner_axes else ()
        if held:
            first = pl.program_id(held[0]) == 0
            for axis in held[1:]:
                first = jnp.logical_and(first, pl.program_id(axis) == 0)
            pl.when(first)(cast)
        else:
            cast()


def _row_blocks(total, size):
    return [slice(r, r + size) for r in range(0, total, size)]


def _norm_matmul_kernel(x_ref, g_ref, w_ref, o_ref, h_ref, *, lead_blocks, lead_scale, sub):
    j = pl.program_id(1)

    def project(h):
        acc = _dot(h, w_ref[...])
        if lead_blocks:
            acc = acc * jnp.where(j < lead_blocks, lead_scale, 1.0)
        return acc.astype(o_ref.dtype)

    @pl.when(j == 0)
    def _():
        for rows in _row_blocks(x_ref.shape[0], sub):
            h = _rms(x_ref[rows, :], g_ref[...]).astype(BF16)
            h_ref[rows, :] = h
            o_ref[rows, :] = project(h)

    @pl.when(j != 0)
    def _():
        o_ref[...] = project(h_ref[...])


def norm_matmul(x, gain, w, layer, out_dtype, tm, tn, lead_cols=0, lead_scale=1.0):
    m, k = x.shape
    n = w.shape[2]
    assert lead_cols % tn == 0
    return pl.pallas_call(
        functools.partial(_norm_matmul_kernel, lead_blocks=lead_cols // tn, lead_scale=lead_scale,
                          sub=min(NORM_SUB_ROWS, tm)),
        grid=(m // tm, n // tn),
        in_specs=[
            pl.BlockSpec((tm, k), lambda i, j: (i, 0)),
            pl.BlockSpec((1, k), lambda i, j: (0, 0)),
            pl.BlockSpec((None, k, tn), lambda i, j: (layer, 0, j)),
        ],
        out_specs=pl.BlockSpec((tm, tn), lambda i, j: (i, j)),
        out_shape=jax.ShapeDtypeStruct((m, n), out_dtype),
        scratch_shapes=[pltpu.VMEM((tm, k), BF16)],
        compiler_params=_params("parallel", "arbitrary"),
        name="norm_matmul",
    )(x, gain, w)


def _ab_proj_kernel(x_ref, g_ref, w_ref, wlow_ref, wgk_ref, bgk_ref, o_ref, gk_ref, h_ref, *, sub):
    @pl.when(pl.program_id(1) == 0)
    def _():
        for rows in _row_blocks(x_ref.shape[0], sub):
            h = _rms(x_ref[rows, :], g_ref[...]).astype(BF16)
            h_ref[rows, :] = h
            o_ref[rows, :] = _dot(h, w_ref[...])
            low = _dot(h, wlow_ref[...])
            pre = _dot(low.astype(BF16), wgk_ref[...]) + bgk_ref[...]
            gk_ref[rows, :] = -_softplus(-pre) * (LOG2_E / GLA_GATE_NORMALIZER)

    @pl.when(pl.program_id(1) != 0)
    def _():
        o_ref[...] = _dot(h_ref[...], w_ref[...])


def ab_proj(x, gain, w, layer, n, w_low, w_gk, b_gk, tm, tn):
    m, k = x.shape
    assert n % tn == 0
    ngk = w_gk.shape[1]
    return pl.pallas_call(
        functools.partial(_ab_proj_kernel, sub=min(NORM_SUB_ROWS, tm)),
        grid=(m // tm, n // tn),
        in_specs=[
            pl.BlockSpec((tm, k), lambda i, j: (i, 0)),
            pl.BlockSpec((1, k), lambda i, j: (0, 0)),
            pl.BlockSpec((None, k, tn), lambda i, j: (layer, 0, j)),
            pl.BlockSpec((k, LANES), lambda i, j: (0, 0)),
            pl.BlockSpec((LANES, ngk), lambda i, j: (0, 0)),
            pl.BlockSpec((1, ngk), lambda i, j: (0, 0)),
        ],
        out_specs=[
            pl.BlockSpec((tm, tn), lambda i, j: (i, j)),
            pl.BlockSpec((tm, ngk), lambda i, j: (i, 0)),
        ],
        out_shape=[
            jax.ShapeDtypeStruct((m, n), F32),
            jax.ShapeDtypeStruct((m, ngk), F32),
        ],
        scratch_shapes=[pltpu.VMEM((tm, k), BF16)],
        compiler_params=_params("parallel", "arbitrary"),
        name="ab_proj",
    )(x, gain, w, w_low, w_gk, b_gk)


def _matmul_resid_kernel(a_ref, w_ref, r_ref, o_ref):
    o_ref[...] = r_ref[...] + _dot(a_ref[...], w_ref[...])


def matmul_resid(a, w, layer, resid, tm, tn):
    m, k = a.shape
    n = w.shape[2]
    return pl.pallas_call(
        _matmul_resid_kernel,
        grid=(m // tm, n // tn),
        in_specs=[
            pl.BlockSpec((tm, k), lambda i, j: (i, 0)),
            pl.BlockSpec((None, k, tn), lambda i, j: (layer, 0, j)),
            pl.BlockSpec((tm, tn), lambda i, j: (i, j)),
        ],
        out_specs=pl.BlockSpec((tm, tn), lambda i, j: (i, j)),
        out_shape=jax.ShapeDtypeStruct((m, n), F32),
        compiler_params=_params("parallel", "arbitrary"),
        name="matmul_resid",
    )(a, w, resid)


def _matmul_resid_norm_kernel(a_ref, w_ref, r_ref, g_ref, o_ref, *, sub):
    for rows in _row_blocks(a_ref.shape[0], sub):
        y = r_ref[rows, :] + _dot(a_ref[rows, :], w_ref[...])
        o_ref[rows, :] = _rms(y, g_ref[...])


def matmul_resid_norm(a, w, layer, resid, gain, tm):
    m, k = a.shape
    n = w.shape[2]
    return pl.pallas_call(
        functools.partial(_matmul_resid_norm_kernel, sub=tm // 2),
        grid=(m // tm,),
        in_specs=[
            pl.BlockSpec((tm, k), lambda i: (i, 0)),
            pl.BlockSpec((None, k, n), lambda i: (layer, 0, 0), pipeline_mode=pl.Buffered(1)),
            pl.BlockSpec((tm, n), lambda i: (i, 0)),
            pl.BlockSpec((1, n), lambda i: (0, 0)),
        ],
        out_specs=pl.BlockSpec((tm, n), lambda i: (i, 0)),
        out_shape=jax.ShapeDtypeStruct((m, n), F32),
        compiler_params=_params("parallel"),
        name="matmul_resid_norm",
    )(a, w, resid, gain)


def _matmul2_resid_kernel(a1_ref, a2_ref, w1_ref, w2_ref, r_ref, o_ref):
    o_ref[...] = r_ref[...] + (_dot(a1_ref[...], w1_ref[...]) + _dot(a2_ref[...], w2_ref[...]))


def matmul2_resid(a1, a2, w, layer, resid, tm, tn):
    m, kh = a1.shape
    n = w.shape[2]
    return pl.pallas_call(
        _matmul2_resid_kernel,
        grid=(m // tm, n // tn),
        in_specs=[
            pl.BlockSpec((tm, kh), lambda i, j: (i, 0)),
            pl.BlockSpec((tm, kh), lambda i, j: (i, 0)),
            pl.BlockSpec((None, kh, tn), lambda i, j: (layer, 0, j)),
            pl.BlockSpec((None, kh, tn), lambda i, j: (layer, 1, j)),
            pl.BlockSpec((tm, tn), lambda i, j: (i, j)),
        ],
        out_specs=pl.BlockSpec((tm, tn), lambda i, j: (i, j)),
        out_shape=jax.ShapeDtypeStruct((m, n), F32),
        compiler_params=_params("parallel", "arbitrary"),
        name="matmul2_resid",
    )(a1, a2, w, w, resid)


def _linear_attention_block(q, k, v, g2, state_ref):
    n = q.shape[0] // CHUNK
    row = lax.broadcasted_iota(jnp.int32, (CHUNK, CHUNK), 0)
    col = lax.broadcasted_iota(jnp.int32, (CHUNK, CHUNK), 1)
    causal = row >= col
    tri = causal.astype(BF16)
    sl = [slice(c * CHUNK, (c + 1) * CHUNK) for c in range(n)]

    g_terms = [_split3(g2[s]) for s in sl]
    b = [_dot(tri, hi) + _dot(tri, mid) + _dot(tri, lo) for hi, mid, lo in g_terms]
    b_mid = [x[CHUNK // 2 - 1:CHUNK // 2] for x in b]
    b_last = [x[CHUNK - 1:] for x in b]
    vb = [v[s].astype(BF16) for s in sl]
    qs = [(q[s] * jnp.exp2(x - m)).astype(BF16) for s, x, m in zip(sl, b, b_mid)]
    ks = [(k[s] * jnp.exp2(m - x)).astype(BF16) for s, x, m in zip(sl, b, b_mid)]
    kd = [(k[s] * jnp.exp2(l - x)).astype(BF16) for s, x, l in zip(sl, b, b_last)]
    qd = [(q[s] * jnp.exp2(x)).astype(BF16) for s, x in zip(sl, b)]

    scores = [jnp.where(causal, _dot_nt(a, c), 0.0).astype(BF16) for a, c in zip(qs, ks)]
    update = [_dot_tn(a, c) for a, c in zip(vb, kd)]
    o_intra = [_dot(a, c) for a, c in zip(scores, vb)]

    state = state_ref[...]
    states = []
    for c in range(n):
        states.append(state.astype(BF16))
        state = state * jnp.exp2(b_last[c]) + update[c]
    state_ref[...] = state

    o = [oi + _dot_nt(a, st) for oi, a, st in zip(o_intra, qd, states)]
    return jnp.concatenate(o, axis=0)


def _hgrn_kernel(*refs, lb_rows, n_riders):
    aq_ref, af_ref, ai_ref, ag_ref, lbl_ref, gain_ref = refs[:6]
    o_ref, state_ref = refs[6 + n_riders], refs[-1]
    _cast_riders(refs[6:6 + n_riders], refs[7 + n_riders:-1])

    @pl.when(pl.program_id(2) == 0)
    def _():
        state_ref[...] = jnp.zeros_like(state_ref)

    logits = lbl_ref[...]
    e = jnp.exp(logits - jnp.max(logits, axis=0, keepdims=True))
    lb = jnp.sum(e[:lb_rows], axis=0, keepdims=True) / jnp.sum(e, axis=0, keepdims=True)

    a_q = aq_ref[0]
    f = lb + (1.0 - lb) * _sigmoid(af_ref[0])
    o = _linear_attention_block(a_q * _sigmoid(a_q), 1.0 - f, ai_ref[0], jnp.log(f) * LOG2_E,
                                state_ref)
    o_ref[0] = (_rms(o, gain_ref[...]) * _sigmoid(ag_ref[0])).astype(o_ref.dtype)


def hgrn_mixer(p, lb_logits, a_idx, gain, ts, riders=()):
    b, s, _ = p.shape
    nh, hd = HGRN_HEADS, HGRN_HEAD_DIM
    nlb = lb_logits.shape[0]
    grid = (b, nh, s // ts)
    r_in, r_out, r_shapes = _rider_specs(riders, grid)

    def col(part):
        return pl.BlockSpec((1, ts, hd), lambda bi, h, si: (bi, si, part * nh + h))

    o, *cast = pl.pallas_call(
        functools.partial(_hgrn_kernel, lb_rows=a_idx + 1, n_riders=len(riders)),
        grid=grid,
        in_specs=[
            col(0), col(1), col(2), col(3),
            pl.BlockSpec((nlb, hd), lambda bi, h, si: (0, h)),
            pl.BlockSpec((1, hd), lambda bi, h, si: (0, 0)),
        ] + r_in,
        out_specs=[pl.BlockSpec((1, ts, hd), lambda bi, h, si: (bi, si, h))] + r_out,
        out_shape=[jax.ShapeDtypeStruct((b, s, nh * hd), BF16)] + r_shapes,
        scratch_shapes=[pltpu.VMEM((hd, hd), F32)],
        compiler_params=_params("parallel", "parallel", "arbitrary"),
        name="hgrn_mixer",
    )(p, p, p, p, lb_logits, gain, *[r.src for r in riders])
    return o, cast


def _gla_kernel(q_ref, k_ref, v_ref, gg_ref, gk_ref, gain_ref, o_ref, state_ref):
    @pl.when(pl.program_id(2) == 0)
    def _():
        state_ref[...] = jnp.zeros_like(state_ref)

    q = q_ref[0] * (GLA_HEAD_K ** -0.5)
    o = _linear_attention_block(q, k_ref[0], v_ref[0], gk_ref[0], state_ref)
    g_g = gg_ref[0]
    o_ref[0] = (_rms(o, gain_ref[...]) * (g_g * _sigmoid(g_g))).astype(o_ref.dtype)


def gla_mixer(p, gk, gain, ts):
    b, s, _ = p.shape
    nh, kd, vd = GLA_HEADS, GLA_HEAD_K, GLA_HEAD_V
    base = 4 * HGRN_HEADS * HGRN_HEAD_DIM
    q0 = base // kd
    k0 = q0 + nh
    v0 = (base + 2 * nh * kd) // vd
    g0 = v0 + nh
    return pl.pallas_call(
        _gla_kernel,
        grid=(b, nh, s // ts),
        in_specs=[
            pl.BlockSpec((1, ts, kd), lambda bi, h, si: (bi, si, q0 + h)),
            pl.BlockSpec((1, ts, kd), lambda bi, h, si: (bi, si, k0 + h)),
            pl.BlockSpec((1, ts, vd), lambda bi, h, si: (bi, si, v0 + h)),
            pl.BlockSpec((1, ts, vd), lambda bi, h, si: (bi, si, g0 + h)),
            pl.BlockSpec((1, ts, kd), lambda bi, h, si: (bi, si, h)),
            pl.BlockSpec((1, vd), lambda bi, h, si: (0, 0)),
        ],
        out_specs=pl.BlockSpec((1, ts, vd), lambda bi, h, si: (bi, si, h)),
        out_shape=jax.ShapeDtypeStruct((b, s, nh * vd), BF16),
        scratch_shapes=[pltpu.VMEM((vd, kd), F32)],
        compiler_params=_params("parallel", "parallel", "arbitrary"),
        name="gla_mixer",
    )(p, p, p, p, gk, gain)


def _sb_kernel(*refs, tq, heads, n_riders):
    q_ref, k_ref, v_ref = refs[:3]
    o_ref = refs[3 + n_riders]
    _cast_riders(refs[3:3 + n_riders], refs[4 + n_riders:])

    qi = pl.program_id(2)
    hd = SB_HEAD_DIM
    row = lax.broadcasted_iota(jnp.int32, (tq, tq), 0)
    col = lax.broadcasted_iota(jnp.int32, (tq, tq), 1)
    behind_ones = (row > col).astype(BF16)
    strictly_causal = col < row
    head_cols = [slice(h * hd, (h + 1) * hd) for h in range(heads)]

    def key_block(j, carry, diagonal):
        start = pl.multiple_of(j * tq, tq)
        acc = [c[0] for c in carry]
        later = [c[1] for c in carry]
        z2 = [_dot_nt(q_ref[0, :, hc], k_ref[0, pl.ds(start, tq), hc]) for hc in head_cols]
        sp2 = [jnp.maximum(z, jnp.log(1.0 + jnp.exp2(jnp.minimum(z, EXP2_CLAMP))) * LOG2_E)
               for z in z2]
        if diagonal:
            sp2 = [jnp.where(strictly_causal, x, 0.0) for x in sp2]
        own = [z - x - lt for z, x, lt in zip(z2, sp2, later)]
        behind = [_dot(x.astype(BF16), behind_ones) for x in sp2]
        log2_a = [o - bh for o, bh in zip(own, behind)]
        if diagonal:
            log2_a = [jnp.where(strictly_causal, x, -jnp.inf) for x in log2_a]
        att = [jnp.exp2(x).astype(BF16) for x in log2_a]
        acc = [a + _dot(p, v_ref[0, pl.ds(start, tq), hc]) for a, p, hc in zip(acc, att, head_cols)]
        later = [lt + (x[:, 0:1] + bh[:, 0:1]) for lt, x, bh in zip(later, sp2, behind)]
        return tuple(zip(acc, later))

    zero = (jnp.zeros((tq, hd), F32), jnp.zeros((tq, 1), F32))
    carry = key_block(qi, (zero,) * heads, True)
    carry = lax.fori_loop(0, qi, lambda t, c: key_block(qi - 1 - t, c, False), carry)
    for h, hc in enumerate(head_cols):
        o_ref[0, :, hc] = carry[h][0].astype(o_ref.dtype)


def sb_attention(qkv, tq, heads, riders=()):
    b, s, _ = qkv.shape
    ng = SB_HEADS // heads
    w = heads * SB_HEAD_DIM
    grid = (b, ng, s // tq)
    r_in, r_out, r_shapes = _rider_specs(riders, grid)
    o, *cast = pl.pallas_call(
        functools.partial(_sb_kernel, tq=tq, heads=heads, n_riders=len(riders)),
        grid=grid,
        in_specs=[
            pl.BlockSpec((1, tq, w), lambda bi, g, qi: (bi, qi, g)),
            pl.BlockSpec((1, s, w), lambda bi, g, qi: (bi, 0, ng + g)),
            pl.BlockSpec((1, s, w), lambda bi, g, qi: (bi, 0, 2 * ng + g)),
        ] + r_in,
        out_specs=[pl.BlockSpec((1, tq, w), lambda bi, g, qi: (bi, qi, g))] + r_out,
        out_shape=[jax.ShapeDtypeStruct((b, s, SB_HEADS * SB_HEAD_DIM), BF16)] + r_shapes,
        compiler_params=_params("parallel", "parallel", "arbitrary"),
        name="sb_attention",
    )(qkv, qkv, qkv, *[r.src for r in riders])
    return o, cast


def _xattn_layer_kernel(x_ref, g_ref, wq_ref, k_ref, v_ref, wo_ref, o_ref):
    x = x_ref[...]
    d = x.shape[-1]
    hd = d // XA_HEADS
    head_cols = [slice(h * hd, (h + 1) * hd) for h in range(XA_HEADS)]
    h_in = [_rms(x[rows], g_ref[...]).astype(BF16)
            for rows in _row_blocks(x.shape[0], min(NORM_SUB_ROWS, x.shape[0]))]
    q = [jnp.concatenate([_dot(hp, wq_ref[:, hc]).astype(BF16) for hp in h_in], axis=0)
         for hc in head_cols]
    scores = [_dot_nt(qh, k_ref[0, :, hc]) * (hd ** -0.5) for qh, hc in zip(q, head_cols)]
    e = [jnp.exp(sc - jnp.max(sc, axis=-1, keepdims=True)) for sc in scores]
    probs = [(eh / jnp.sum(eh, axis=-1, keepdims=True)).astype(BF16) for eh in e]
    o = [_dot(ph, v_ref[0, :, hc]).astype(BF16) for ph, hc in zip(probs, head_cols)]
    o_ref[...] = x + _dot(jnp.concatenate(o, axis=1), wo_ref[...])


def xattn_layer(x, gain, w_q, kv, w_o, layer, seq, tm):
    rows, d = x.shape
    m = kv.shape[1]
    blocks_per_seq = seq // tm
    whole = functools.partial(pl.BlockSpec, (None, d, d), lambda i: (layer, 0, 0),
                              pipeline_mode=pl.Buffered(1))
    return pl.pallas_call(
        _xattn_layer_kernel,
        grid=(rows // tm,),
        in_specs=[
            pl.BlockSpec((tm, d), lambda i: (i, 0)),
            pl.BlockSpec((1, d), lambda i: (0, 0)),
            whole(),
            pl.BlockSpec((1, m, d), lambda i: (i // blocks_per_seq, 0, 0)),
            pl.BlockSpec((1, m, d), lambda i: (i // blocks_per_seq, 0, 1)),
            whole(),
        ],
        out_specs=pl.BlockSpec((tm, d), lambda i: (i, 0)),
        out_shape=jax.ShapeDtypeStruct((rows, d), F32),
        compiler_params=_params("parallel"),
        name="xattn_layer",
    )(x, gain, w_q, kv, kv, w_o)


HALO = BF16_SUBLANES


def _ffn_in_kernel(*refs, subs, blocks_per_seq, rider_inner_axes):
    n_riders = len(rider_inner_axes)
    x_ref, halo_ref, g_ref, wu_ref, wg_ref, cw_ref, cb_ref = refs[:7]
    o_ref, h_ref, gs_ref = refs[7 + n_riders], refs[-2], refs[-1]
    _cast_riders(refs[7:7 + n_riders], refs[8 + n_riders:-2], rider_inner_axes)

    inside = pl.program_id(0) % blocks_per_seq != 0
    cw = cw_ref[...]
    cb = cb_ref[...]

    def row_blocks(normalise):
        start = 0
        for sub in subs:
            lo = HALO + start
            first = 0 if start == 0 else lo
            if normalise:
                h_ref[lo:lo + sub, :] = _rms(x_ref[start:start + sub, :], g_ref[...]).astype(BF16)
            gs_ref[first:lo + sub, :] = _dot(h_ref[first:lo + sub, :], wg_ref[...])
            u = _dot(h_ref[lo:lo + sub, :], wu_ref[...])
            conv = 0.0
            for tap in range(CONV_W):
                t0 = lo - (CONV_W - 1) + tap
                conv = conv + cw[tap:tap + 1] * gs_ref[t0:t0 + sub, :]
            gc = cb + conv
            o_ref[start:start + sub, :] = (gc * _sigmoid(gc) * u).astype(o_ref.dtype)
            start += sub

    @pl.when(pl.program_id(1) == 0)
    def _():
        hh = _rms(halo_ref[...], g_ref[...])
        h_ref[:HALO, :] = jnp.where(inside, hh, 0.0).astype(BF16)
        row_blocks(True)

    @pl.when(pl.program_id(1) != 0)
    def _():
        row_blocks(False)


def ffn_in_grid(m, f, tm, tn):
    return (m // tm, f // tn)


def ffn_in(x, gain, w_in, layer, conv_w, conv_b, seq, tm, tn, subs, riders=()):
    m, d = x.shape
    assert sum(subs) == tm
    f = w_in.shape[2] // 2
    grid = ffn_in_grid(m, f, tm, tn)
    nj = grid[1]
    halo_per_block = tm // HALO
    r_in, r_out, r_shapes = _rider_specs(riders, grid)
    o, *cast = pl.pallas_call(
        functools.partial(_ffn_in_kernel, subs=subs, blocks_per_seq=seq // tm,
                          rider_inner_axes=tuple(tuple(range(r.axes, len(grid))) for r in riders)),
        grid=grid,
        in_specs=[
            pl.BlockSpec((tm, d), lambda i, j: (i, 0)),
            pl.BlockSpec((HALO, d), lambda i, j: (jnp.maximum(i * halo_per_block - 1, 0), 0)),
            pl.BlockSpec((1, d), lambda i, j: (0, 0)),
            pl.BlockSpec((None, d, tn), lambda i, j: (layer, 0, j)),
            pl.BlockSpec((None, d, tn), lambda i, j: (layer, 0, nj + j)),
            pl.BlockSpec((CONV_W, tn), lambda i, j: (0, j)),
            pl.BlockSpec((1, tn), lambda i, j: (0, j)),
        ] + r_in,
        out_specs=[pl.BlockSpec((tm, tn), lambda i, j: (i, j))] + r_out,
        out_shape=[jax.ShapeDtypeStruct((m, f), BF16)] + r_shapes,
        scratch_shapes=[pltpu.VMEM((HALO + tm, d), BF16), pltpu.VMEM((HALO + tm, tn), F32)],
        compiler_params=_params("parallel", "arbitrary"),
        name="ffn_in",
    )(x, x, gain, w_in, w_in, conv_w, conv_b, *[r.src for r in riders])
    return o, cast


def kernel(x, mem, mem_norm, norm_mix, norm_xattn, norm_ffn, ab_w_in, hgrn_lb_logits, hgrn_norm, gla_w_gk, gla_b_gk, gla_norm, ab_w_out, sb_w_qkv, sb_w_out, xa_w_q, xa_w_kv, xa_w_o, ffn_w_in, ffn_conv_w, ffn_conv_b, ffn_w_out, final_norm):
    b, s, d = x.shape
    depth = norm_mix.shape[0]
    m_len = mem.shape[1]
    rows = b * s
    tm = min(512, s)
    tm_big = min(1024, rows)
    tm_ffn = min(1024, s)
    ffn_subs = (tm_ffn // 2, tm_ffn // 2)
    ts = min(2048, s)
    tq_sb = min(256, s)

    def row(v):
        return v.reshape(1, -1)

    xr = x.reshape(rows, d)

    stacks = dict(ab_w_in=ab_w_in, ab_w_out=ab_w_out, sb_w_qkv=sb_w_qkv, sb_w_out=sb_w_out,
                  xa_w_q=xa_w_q, xa_w_kv=xa_w_kv, xa_w_o=xa_w_o, ffn_w_in=ffn_w_in,
                  ffn_w_out=ffn_w_out)
    ready = {}

    def weight(name, idx):
        if (name, idx) not in ready:
            ready[(name, idx)] = stacks[name][idx:idx + 1].astype(BF16)
        return ready[(name, idx)]

    def plan(wanted, grid, axes=None):
        keys = [key for key in wanted if key not in ready and key[1] < stacks[key[0]].shape[0]]
        riders = [_plan_rider(stacks[name], idx, grid, axes) for name, idx in keys]
        keys = [key for key, r in zip(keys, riders) if r is not None]
        return [r for r in riders if r is not None], keys

    def adopt(keys, cast):
        for key, w_bf16 in zip(keys, cast):
            ready[key] = w_bf16[None]

    ab_main = 4 * HGRN_HEADS * HGRN_HEAD_DIM + GLA_HEADS * (2 * GLA_HEAD_K + 2 * GLA_HEAD_V)
    mem_rows = mem.reshape(b * m_len, d)
    d_ff = ffn_w_in.shape[2] // 2

    for layer in range(depth):
        gain = row(norm_mix[layer])
        xattn_weights = [("xa_w_q", layer), ("xa_w_kv", layer), ("xa_w_o", layer)]
        if layer % 2 == 0:
            a = layer // 2
            w_low = jnp.pad(ab_w_in[a][:, ab_main:], ((0, 0), (0, LANES - GLA_GATE_RANK))).astype(BF16)
            w_gk = jnp.pad(gla_w_gk[a], ((0, LANES - GLA_GATE_RANK), (0, 0))).astype(BF16)
            p, gk = ab_proj(xr, gain, weight("ab_w_in", a), 0, ab_main, w_low, w_gk,
                            row(gla_b_gk[a]), tm=tm_big, tn=1024)
            p = p.reshape(b, s, ab_main)
            riders, keys = plan([("ab_w_out", a)] + xattn_weights, (b, HGRN_HEADS, s // ts))
            o_a, cast = hgrn_mixer(p, hgrn_lb_logits, a, row(hgrn_norm[a]), ts, riders)
            adopt(keys, cast)
            o_b = gla_mixer(p, gk.reshape(b, s, -1), row(gla_norm[a]), ts)
            xr = matmul2_resid(o_a.reshape(rows, -1), o_b.reshape(rows, -1),
                               weight("ab_w_out", a), 0, xr, tm=tm, tn=d)
        else:
            c = layer // 2
            qkv = norm_matmul(xr, gain, weight("sb_w_qkv", c), 0, BF16, tm=tm_big, tn=1024,
                              lead_cols=d, lead_scale=(SB_HEAD_DIM ** -0.5) * LOG2_E)
            riders, keys = plan([("sb_w_out", c)] + xattn_weights
                                + [("ffn_w_in", layer), ("ffn_w_out", layer)],
                                (b, SB_HEADS // SB_HEADS_PER_STEP, s // tq_sb))
            o, cast = sb_attention(qkv.reshape(b, s, -1), tq_sb, SB_HEADS_PER_STEP, riders)
            adopt(keys, cast)
            xr = matmul_resid(o.reshape(rows, d), weight("sb_w_out", c), 0, xr, tm=tm, tn=d)

        kv = norm_matmul(mem_rows, row(mem_norm), weight("xa_w_kv", layer), 0, BF16,
                         tm=min(1024, b * m_len), tn=d)
        xr = xattn_layer(xr, row(norm_xattn[layer]), weight("xa_w_q", layer),
                         kv.reshape(b, m_len, 2 * d), weight("xa_w_o", layer), 0, seq=s, tm=tm)

        ffn_grid = ffn_in_grid(rows, d_ff, tm_ffn, 512)
        riders, keys = plan([("ffn_w_out", layer)], ffn_grid)
        first_next = ("sb_w_qkv", layer // 2) if layer % 2 == 0 else ("ab_w_in", layer // 2 + 1)
        riders_next, keys_next = plan([first_next], ffn_grid, axes=1)
        act, cast = ffn_in(xr, row(norm_ffn[layer]), weight("ffn_w_in", layer), 0,
                           ffn_conv_w[layer], row(ffn_conv_b[layer]), seq=s, tm=tm_ffn, tn=512,
                           subs=ffn_subs, riders=riders + riders_next)
        adopt(keys + keys_next, cast)
        if layer < depth - 1:
            xr = matmul_resid(act, weight("ffn_w_out", layer), 0, xr, tm=tm_big, tn=512)
        else:
            xr = matmul_resid_norm(act, weight("ffn_w_out", layer), 0, xr, row(final_norm),
                                   tm=min(256, rows))

    return xr.reshape(b, s, d)
```

```python
import functools
from typing import NamedTuple

import jax
import jax.numpy as jnp
from jax import lax
from jax.experimental import pallas as pl
from jax.experimental.pallas import tpu as pltpu

F32 = jnp.float32
BF16 = jnp.bfloat16

RMS_EPS = 1e-6
CHUNK = 64
HGRN_HEADS = 8
HGRN_HEAD_DIM = 128
GLA_HEADS = 4
GLA_HEAD_K = 128
GLA_HEAD_V = 256
GLA_GATE_RANK = 16
GLA_GATE_NORMALIZER = 16.0
SB_HEADS = 16
SB_HEAD_DIM = 128
XA_HEADS = 4
CONV_W = 3
LOG2_E = 1.4426950408889634
EXP2_CLAMP = 126.0
SB_HEADS_PER_STEP = 4

LANES = 128
BF16_SUBLANES = 16
VMEM_LIMIT_BYTES = 52 * 1024 * 1024
NORM_SUB_ROWS = 256


def _params(*semantics):
    return pltpu.CompilerParams(dimension_semantics=semantics, vmem_limit_bytes=VMEM_LIMIT_BYTES)


def _rms(x, gain):
    return x * lax.rsqrt(jnp.mean(x * x, axis=-1, keepdims=True) + RMS_EPS) * gain


def _sigmoid(x):
    return 1.0 / (1.0 + jnp.exp(-x))


def _softplus(x):
    return jnp.maximum(x, 0.0) + jnp.log(1.0 + jnp.exp(-jnp.abs(x)))


def _dot(a, b):
    return jnp.dot(a, b, preferred_element_type=F32)


def _dot_nt(a, b):
    return lax.dot_general(a, b, (((1,), (1,)), ((), ())), preferred_element_type=F32)


def _dot_tn(a, b):
    return lax.dot_general(a, b, (((0,), (0,)), ((), ())), preferred_element_type=F32)


def _split3(x):
    hi = x.astype(BF16)
    r1 = x - hi.astype(F32)
    mid = r1.astype(BF16)
    lo = (r1 - mid.astype(F32)).astype(BF16)
    return hi, mid, lo


class _Rider(NamedTuple):
    src: jax.Array
    layer: int
    row_blocks: int
    col_blocks: int
    axes: int


def _plan_rider(src, layer, grid, axes=None):
    axes = len(grid) if axes is None else axes
    steps = 1
    for g in grid[:axes]:
        steps *= g
    k, n = src.shape[-2:]
    for col_blocks in range(1, steps + 1):
        if steps % col_blocks or n % (col_blocks * LANES):
            continue
        row_blocks = steps // col_blocks
        if k % (row_blocks * BF16_SUBLANES) == 0:
            return _Rider(src, layer, row_blocks, col_blocks, axes)
    return None


def _rider_specs(riders, grid):
    def step(ids):
        lin = ids[0]
        for g, i in zip(grid[1:len(ids)], ids[1:]):
            lin = lin * g + i
        return lin

    in_specs, out_specs, out_shapes = [], [], []
    for r in riders:
        k, n = r.src.shape[-2:]
        block = (k // r.row_blocks, n // r.col_blocks)

        def block_index(*ids, cb=r.col_blocks, axes=r.axes):
            lin = step(ids[:axes])
            return lin // cb, lin % cb

        def src_index(*ids, layer=r.layer, f=block_index):
            return (layer,) + tuple(f(*ids))

        in_specs.append(pl.BlockSpec((None,) + block, src_index))
        out_specs.append(pl.BlockSpec(block, block_index))
        out_shapes.append(jax.ShapeDtypeStruct((k, n), BF16))
    return in_specs, out_specs, out_shapes


def _cast_riders(src_refs, dst_refs, inner_axes=()):
    for n, (src_ref, dst_ref) in enumerate(zip(src_refs, dst_refs)):
        def cast(src_ref=src_ref, dst_ref=dst_ref):
            dst_ref[...] = src_ref[...].astype(BF16)

        held = inner_axes[n] if inner_axes else ()
        if held:
            first = pl.program_id(held[0]) == 0
            for axis in held[1:]:
                first = jnp.logical_and(first, pl.program_id(axis) == 0)
            pl.when(first)(cast)
        else:
            cast()


def _row_blocks(total, size):
    return [slice(r, r + size) for r in range(0, total, size)]


def _norm_matmul_kernel(x_ref, g_ref, w_ref, o_ref, h_ref, *, lead_blocks, lead_scale, sub):
    j = pl.program_id(1)

    def project(h):
        acc = _dot(h, w_ref[...])
        if lead_blocks:
            acc = acc * jnp.where(j < lead_blocks, lead_scale, 1.0)
        return acc.astype(o_ref.dtype)

    @pl.when(j == 0)
    def _():
        for rows in _row_blocks(x_ref.shape[0], sub):
            h = _rms(x_ref[rows, :], g_ref[...]).astype(BF16)
            h_ref[rows, :] = h
            o_ref[rows, :] = project(h)

    @pl.when(j != 0)
    def _():
        o_ref[...] = project(h_ref[...])


def norm_matmul(x, gain, w, layer, out_dtype, tm, tn, lead_cols=0, lead_scale=1.0):
    m, k = x.shape
    n = w.shape[2]
    assert lead_cols % tn == 0
    return pl.pallas_call(
        functools.partial(_norm_matmul_kernel, lead_blocks=lead_cols // tn, lead_scale=lead_scale,
                          sub=min(NORM_SUB_ROWS, tm)),
        grid=(m // tm, n // tn),
        in_specs=[
            pl.BlockSpec((tm, k), lambda i, j: (i, 0)),
            pl.BlockSpec((1, k), lambda i, j: (0, 0)),
            pl.BlockSpec((None, k, tn), lambda i, j: (layer, 0, j)),
        ],
        out_specs=pl.BlockSpec((tm, tn), lambda i, j: (i, j)),
        out_shape=jax.ShapeDtypeStruct((m, n), out_dtype),
        scratch_shapes=[pltpu.VMEM((tm, k), BF16)],
        compiler_params=_params("parallel", "arbitrary"),
        name="norm_matmul",
    )(x, gain, w)


def _ab_proj_kernel(x_ref, g_ref, w_ref, wlow_ref, wgk_ref, bgk_ref, o_ref, gk_ref, h_ref, *, sub):
    @pl.when(pl.program_id(1) == 0)
    def _():
        for rows in _row_blocks(x_ref.shape[0], sub):
            h = _rms(x_ref[rows, :], g_ref[...]).astype(BF16)
            h_ref[rows, :] = h
            o_ref[rows, :] = _dot(h, w_ref[...])
            low = _dot(h, wlow_ref[...])
            pre = _dot(low.astype(BF16), wgk_ref[...]) + bgk_ref[...]
            gk_ref[rows, :] = -_softplus(-pre) * (LOG2_E / GLA_GATE_NORMALIZER)

    @pl.when(pl.program_id(1) != 0)
    def _():
        o_ref[...] = _dot(h_ref[...], w_ref[...])


def ab_proj(x, gain, w, layer, n, w_low, w_gk, b_gk, tm, tn):
    m, k = x.shape
    assert n % tn == 0
    ngk = w_gk.shape[1]
    return pl.pallas_call(
        functools.partial(_ab_proj_kernel, sub=min(NORM_SUB_ROWS, tm)),
        grid=(m // tm, n // tn),
        in_specs=[
            pl.BlockSpec((tm, k), lambda i, j: (i, 0)),
            pl.BlockSpec((1, k), lambda i, j: (0, 0)),
            pl.BlockSpec((None, k, tn), lambda i, j: (layer, 0, j)),
            pl.BlockSpec((k, LANES), lambda i, j: (0, 0)),
            pl.BlockSpec((LANES, ngk), lambda i, j: (0, 0)),
            pl.BlockSpec((1, ngk), lambda i, j: (0, 0)),
        ],
        out_specs=[
            pl.BlockSpec((tm, tn), lambda i, j: (i, j)),
            pl.BlockSpec((tm, ngk), lambda i, j: (i, 0)),
        ],
        out_shape=[
            jax.ShapeDtypeStruct((m, n), F32),
            jax.ShapeDtypeStruct((m, ngk), F32),
        ],
        scratch_shapes=[pltpu.VMEM((tm, k), BF16)],
        compiler_params=_params("parallel", "arbitrary"),
        name="ab_proj",
    )(x, gain, w, w_low, w_gk, b_gk)


def _matmul_resid_kernel(a_ref, w_ref, r_ref, o_ref):
    o_ref[...] = r_ref[...] + _dot(a_ref[...], w_ref[...])


def matmul_resid(a, w, layer, resid, tm, tn):
    m, k = a.shape
    n = w.shape[2]
    return pl.pallas_call(
        _matmul_resid_kernel,
        grid=(m // tm, n // tn),
        in_specs=[
            pl.BlockSpec((tm, k), lambda i, j: (i, 0)),
            pl.BlockSpec((None, k, tn), lambda i, j: (layer, 0, j)),
            pl.BlockSpec((tm, tn), lambda i, j: (i, j)),
        ],
        out_specs=pl.BlockSpec((tm, tn), lambda i, j: (i, j)),
        out_shape=jax.ShapeDtypeStruct((m, n), F32),
        compiler_params=_params("parallel", "arbitrary"),
        name="matmul_resid",
    )(a, w, resid)


def _matmul_resid_norm_kernel(a_ref, w_ref, r_ref, g_ref, o_ref, *, sub):
    for rows in _row_blocks(a_ref.shape[0], sub):
        y = r_ref[rows, :] + _dot(a_ref[rows, :], w_ref[...])
        o_ref[rows, :] = _rms(y, g_ref[...])


def matmul_resid_norm(a, w, layer, resid, gain, tm):
    m, k = a.shape
    n = w.shape[2]
    return pl.pallas_call(
        functools.partial(_matmul_resid_norm_kernel, sub=tm // 2),
        grid=(m // tm,),
        in_specs=[
            pl.BlockSpec((tm, k), lambda i: (i, 0)),
            pl.BlockSpec((None, k, n), lambda i: (layer, 0, 0), pipeline_mode=pl.Buffered(1)),
            pl.BlockSpec((tm, n), lambda i: (i, 0)),
            pl.BlockSpec((1, n), lambda i: (0, 0)),
        ],
        out_specs=pl.BlockSpec((tm, n), lambda i: (i, 0)),
        out_shape=jax.ShapeDtypeStruct((m, n), F32),
        compiler_params=_params("parallel"),
        name="matmul_resid_norm",
    )(a, w, resid, gain)


def _matmul2_resid_kernel(a1_ref, a2_ref, w1_ref, w2_ref, r_ref, o_ref):
    o_ref[...] = r_ref[...] + (_dot(a1_ref[...], w1_ref[...]) + _dot(a2_ref[...], w2_ref[...]))


def matmul2_resid(a1, a2, w, layer, resid, tm, tn):
    m, kh = a1.shape
    n = w.shape[2]
    return pl.pallas_call(
        _matmul2_resid_kernel,
        grid=(m // tm, n // tn),
        in_specs=[
            pl.BlockSpec((tm, kh), lambda i, j: (i, 0)),
            pl.BlockSpec((tm, kh), lambda i, j: (i, 0)),
            pl.BlockSpec((None, kh, tn), lambda i, j: (layer, 0, j)),
            pl.BlockSpec((None, kh, tn), lambda i, j: (layer, 1, j)),
            pl.BlockSpec((tm, tn), lambda i, j: (i, j)),
        ],
        out_specs=pl.BlockSpec((tm, tn), lambda i, j: (i, j)),
        out_shape=jax.ShapeDtypeStruct((m, n), F32),
        compiler_params=_params("parallel", "arbitrary"),
        name="matmul2_resid",
    )(a1, a2, w, w, resid)


def _linear_attention_block(q, k, v, g2, state_ref):
    n = q.shape[0] // CHUNK
    row = lax.broadcasted_iota(jnp.int32, (CHUNK, CHUNK), 0)
    col = lax.broadcasted_iota(jnp.int32, (CHUNK, CHUNK), 1)
    causal = row >= col
    tri = causal.astype(BF16)
    sl = [slice(c * CHUNK, (c + 1) * CHUNK) for c in range(n)]

    g_terms = [_split3(g2[s]) for s in sl]
    b = [_dot(tri, hi) + _dot(tri, mid) + _dot(tri, lo) for hi, mid, lo in g_terms]
    b_mid = [x[CHUNK // 2 - 1:CHUNK // 2] for x in b]
    b_last = [x[CHUNK - 1:] for x in b]
    vb = [v[s].astype(BF16) for s in sl]
    qs = [(q[s] * jnp.exp2(x - m)).astype(BF16) for s, x, m in zip(sl, b, b_mid)]
    ks = [(k[s] * jnp.exp2(m - x)).astype(BF16) for s, x, m in zip(sl, b, b_mid)]
    kd = [(k[s] * jnp.exp2(l - x)).astype(BF16) for s, x, l in zip(sl, b, b_last)]
    qd = [(q[s] * jnp.exp2(x)).astype(BF16) for s, x in zip(sl, b)]

    scores = [jnp.where(causal, _dot_nt(a, c), 0.0).astype(BF16) for a, c in zip(qs, ks)]
    update = [_dot_tn(a, c) for a, c in zip(vb, kd)]
    o_intra = [_dot(a, c) for a, c in zip(scores, vb)]

    state = state_ref[...]
    states = []
    for c in range(n):
        states.append(state.astype(BF16))
        state = state * jnp.exp2(b_last[c]) + update[c]
    state_ref[...] = state

    o = [oi + _dot_nt(a, st) for oi, a, st in zip(o_intra, qd, states)]
    return jnp.concatenate(o, axis=0)


def _hgrn_kernel(*refs, lb_rows, n_riders):
    aq_ref, af_ref, ai_ref, ag_ref, lbl_ref, gain_ref = refs[:6]
    o_ref, state_ref = refs[6 + n_riders], refs[-1]
    _cast_riders(refs[6:6 + n_riders], refs[7 + n_riders:-1])

    @pl.when(pl.program_id(2) == 0)
    def _():
        state_ref[...] = jnp.zeros_like(state_ref)

    logits = lbl_ref[...]
    e = jnp.exp(logits - jnp.max(logits, axis=0, keepdims=True))
    lb = jnp.sum(e[:lb_rows], axis=0, keepdims=True) / jnp.sum(e, axis=0, keepdims=True)

    a_q = aq_ref[0]
    f = lb + (1.0 - lb) * _sigmoid(af_ref[0])
    o = _linear_attention_block(a_q * _sigmoid(a_q), 1.0 - f, ai_ref[0], jnp.log(f) * LOG2_E,
                                state_ref)
    o_ref[0] = (_rms(o, gain_ref[...]) * _sigmoid(ag_ref[0])).astype(o_ref.dtype)


def hgrn_mixer(p, lb_logits, a_idx, gain, ts, riders=()):
    b, s, _ = p.shape
    nh, hd = HGRN_HEADS, HGRN_HEAD_DIM
    nlb = lb_logits.shape[0]
    grid = (b, nh, s // ts)
    r_in, r_out, r_shapes = _rider_specs(riders, grid)

    def col(part):
        return pl.BlockSpec((1, ts, hd), lambda bi, h, si: (bi, si, part * nh + h))

    o, *cast = pl.pallas_call(
        functools.partial(_hgrn_kernel, lb_rows=a_idx + 1, n_riders=len(riders)),
        grid=grid,
        in_specs=[
            col(0), col(1), col(2), col(3),
            pl.BlockSpec((nlb, hd), lambda bi, h, si: (0, h)),
            pl.BlockSpec((1, hd), lambda bi, h, si: (0, 0)),
        ] + r_in,
        out_specs=[pl.BlockSpec((1, ts, hd), lambda bi, h, si: (bi, si, h))] + r_out,
        out_shape=[jax.ShapeDtypeStruct((b, s, nh * hd), BF16)] + r_shapes,
        scratch_shapes=[pltpu.VMEM((hd, hd), F32)],
        compiler_params=_params("parallel", "parallel", "arbitrary"),
        name="hgrn_mixer",
    )(p, p, p, p, lb_logits, gain, *[r.src for r in riders])
    return o, cast


def _gla_kernel(q_ref, k_ref, v_ref, gg_ref, gk_ref, gain_ref, o_ref, state_ref):
    @pl.when(pl.program_id(2) == 0)
    def _():
        state_ref[...] = jnp.zeros_like(state_ref)

    q = q_ref[0] * (GLA_HEAD_K ** -0.5)
    o = _linear_attention_block(q, k_ref[0], v_ref[0], gk_ref[0], state_ref)
    g_g = gg_ref[0]
    o_ref[0] = (_rms(o, gain_ref[...]) * (g_g * _sigmoid(g_g))).astype(o_ref.dtype)


def gla_mixer(p, gk, gain, ts):
    b, s, _ = p.shape
    nh, kd, vd = GLA_HEADS, GLA_HEAD_K, GLA_HEAD_V
    base = 4 * HGRN_HEADS * HGRN_HEAD_DIM
    q0 = base // kd
    k0 = q0 + nh
    v0 = (base + 2 * nh * kd) // vd
    g0 = v0 + nh
    return pl.pallas_call(
        _gla_kernel,
        grid=(b, nh, s // ts),
        in_specs=[
            pl.BlockSpec((1, ts, kd), lambda bi, h, si: (bi, si, q0 + h)),
            pl.BlockSpec((1, ts, kd), lambda bi, h, si: (bi, si, k0 + h)),
            pl.BlockSpec((1, ts, vd), lambda bi, h, si: (bi, si, v0 + h)),
            pl.BlockSpec((1, ts, vd), lambda bi, h, si: (bi, si, g0 + h)),
            pl.BlockSpec((1, ts, kd), lambda bi, h, si: (bi, si, h)),
            pl.BlockSpec((1, vd), lambda bi, h, si: (0, 0)),
        ],
        out_specs=pl.BlockSpec((1, ts, vd), lambda bi, h, si: (bi, si, h)),
        out_shape=jax.ShapeDtypeStruct((b, s, nh * vd), BF16),
        scratch_shapes=[pltpu.VMEM((vd, kd), F32)],
        compiler_params=_params("parallel", "parallel", "arbitrary"),
        name="gla_mixer",
    )(p, p, p, p, gk, gain)


def _sb_kernel(*refs, tq, heads, n_riders):
    q_ref, k_ref, v_ref = refs[:3]
    o_ref = refs[3 + n_riders]
    _cast_riders(refs[3:3 + n_riders], refs[4 + n_riders:])

    qi = pl.program_id(2)
    hd = SB_HEAD_DIM
    row = lax.broadcasted_iota(jnp.int32, (tq, tq), 0)
    col = lax.broadcasted_iota(jnp.int32, (tq, tq), 1)
    behind_ones = (row > col).astype(BF16)
    strictly_causal = col < row
    head_cols = [slice(h * hd, (h + 1) * hd) for h in range(heads)]

    def key_block(j, carry, diagonal):
        start = pl.multiple_of(j * tq, tq)
        acc = [c[0] for c in carry]
        later = [c[1] for c in carry]
        z2 = [_dot_nt(q_ref[0, :, hc], k_ref[0, pl.ds(start, tq), hc]) for hc in head_cols]
        sp2 = [jnp.maximum(z, jnp.log(1.0 + jnp.exp2(jnp.minimum(z, EXP2_CLAMP))) * LOG2_E)
               for z in z2]
        if diagonal:
            sp2 = [jnp.where(strictly_causal, x, 0.0) for x in sp2]
        own = [z - x - lt for z, x, lt in zip(z2, sp2, later)]
        behind = [_dot(x.astype(BF16), behind_ones) for x in sp2]
        log2_a = [o - bh for o, bh in zip(own, behind)]
        if diagonal:
            log2_a = [jnp.where(strictly_causal, x, -jnp.inf) for x in log2_a]
        att = [jnp.exp2(x).astype(BF16) for x in log2_a]
        acc = [a + _dot(p, v_ref[0, pl.ds(start, tq), hc]) for a, p, hc in zip(acc, att, head_cols)]
        later = [lt + (x[:, 0:1] + bh[:, 0:1]) for lt, x, bh in zip(later, sp2, behind)]
        return tuple(zip(acc, later))

    zero = (jnp.zeros((tq, hd), F32), jnp.zeros((tq, 1), F32))
    carry = key_block(qi, (zero,) * heads, True)
    carry = lax.fori_loop(0, qi, lambda t, c: key_block(qi - 1 - t, c, False), carry)
    for h, hc in enumerate(head_cols):
        o_ref[0, :, hc] = carry[h][0].astype(o_ref.dtype)


def sb_attention(qkv, tq, heads, riders=()):
    b, s, _ = qkv.shape
    ng = SB_HEADS // heads
    w = heads * SB_HEAD_DIM
    grid = (b, ng, s // tq)
    r_in, r_out, r_shapes = _rider_specs(riders, grid)
    o, *cast = pl.pallas_call(
        functools.partial(_sb_kernel, tq=tq, heads=heads, n_riders=len(riders)),
        grid=grid,
        in_specs=[
            pl.BlockSpec((1, tq, w), lambda bi, g, qi: (bi, qi, g)),
            pl.BlockSpec((1, s, w), lambda bi, g, qi: (bi, 0, ng + g)),
            pl.BlockSpec((1, s, w), lambda bi, g, qi: (bi, 0, 2 * ng + g)),
        ] + r_in,
        out_specs=[pl.BlockSpec((1, tq, w), lambda bi, g, qi: (bi, qi, g))] + r_out,
        out_shape=[jax.ShapeDtypeStruct((b, s, SB_HEADS * SB_HEAD_DIM), BF16)] + r_shapes,
        compiler_params=_params("parallel", "parallel", "arbitrary"),
        name="sb_attention",
    )(qkv, qkv, qkv, *[r.src for r in riders])
    return o, cast


def _xattn_layer_kernel(*refs, n_riders):
    x_ref, g_ref, wq_ref, k_ref, v_ref, wo_ref = refs[:6]
    o_ref = refs[6 + n_riders]
    _cast_riders(refs[6:6 + n_riders], refs[7 + n_riders:])

    x = x_ref[...]
    d = x.shape[-1]
    hd = d // XA_HEADS
    head_cols = [slice(h * hd, (h + 1) * hd) for h in range(XA_HEADS)]
    h_in = [_rms(x[rows], g_ref[...]).astype(BF16)
            for rows in _row_blocks(x.shape[0], min(NORM_SUB_ROWS, x.shape[0]))]
    q = [jnp.concatenate([_dot(hp, wq_ref[:, hc]).astype(BF16) for hp in h_in], axis=0)
         for hc in head_cols]
    scores = [_dot_nt(qh, k_ref[0, :, hc]) * (hd ** -0.5) for qh, hc in zip(q, head_cols)]
    e = [jnp.exp(sc - jnp.max(sc, axis=-1, keepdims=True)) for sc in scores]
    probs = [(eh / jnp.sum(eh, axis=-1, keepdims=True)).astype(BF16) for eh in e]
    o = [_dot(ph, v_ref[0, :, hc]).astype(BF16) for ph, hc in zip(probs, head_cols)]
    o_ref[...] = x + _dot(jnp.concatenate(o, axis=1), wo_ref[...])


def xattn_layer(x, gain, w_q, kv, w_o, layer, seq, tm, riders=()):
    rows, d = x.shape
    m = kv.shape[1]
    blocks_per_seq = seq // tm
    grid = (rows // tm,)
    r_in, r_out, r_shapes = _rider_specs(riders, grid)
    whole = functools.partial(pl.BlockSpec, (None, d, d), lambda i: (layer, 0, 0),
                              pipeline_mode=pl.Buffered(1))
    o, *cast = pl.pallas_call(
        functools.partial(_xattn_layer_kernel, n_riders=len(riders)),
        grid=grid,
        in_specs=[
            pl.BlockSpec((tm, d), lambda i: (i, 0)),
            pl.BlockSpec((1, d), lambda i: (0, 0)),
            whole(),
            pl.BlockSpec((1, m, d), lambda i: (i // blocks_per_seq, 0, 0)),
            pl.BlockSpec((1, m, d), lambda i: (i // blocks_per_seq, 0, 1)),
            whole(),
        ] + r_in,
        out_specs=[pl.BlockSpec((tm, d), lambda i: (i, 0))] + r_out,
        out_shape=[jax.ShapeDtypeStruct((rows, d), F32)] + r_shapes,
        compiler_params=_params("parallel"),
        name="xattn_layer",
    )(x, gain, w_q, kv, kv, w_o, *[r.src for r in riders])
    return o, cast


HALO = BF16_SUBLANES


def _ffn_in_kernel(*refs, subs, blocks_per_seq, rider_inner_axes):
    n_riders = len(rider_inner_axes)
    x_ref, halo_ref, g_ref, wu_ref, wg_ref, cw_ref, cb_ref = refs[:7]
    o_ref, h_ref, gs_ref = refs[7 + n_riders], refs[-2], refs[-1]
    _cast_riders(refs[7:7 + n_riders], refs[8 + n_riders:-2], rider_inner_axes)

    inside = pl.program_id(0) % blocks_per_seq != 0
    cw = cw_ref[...]
    cb = cb_ref[...]

    def row_blocks(normalise):
        start = 0
        for sub in subs:
            lo = HALO + start
            first = 0 if start == 0 else lo
            if normalise:
                h_ref[lo:lo + sub, :] = _rms(x_ref[start:start + sub, :], g_ref[...]).astype(BF16)
            gs_ref[first:lo + sub, :] = _dot(h_ref[first:lo + sub, :], wg_ref[...])
            u = _dot(h_ref[lo:lo + sub, :], wu_ref[...])
            conv = 0.0
            for tap in range(CONV_W):
                t0 = lo - (CONV_W - 1) + tap
                conv = conv + cw[tap:tap + 1] * gs_ref[t0:t0 + sub, :]
            gc = cb + conv
            o_ref[start:start + sub, :] = (gc * _sigmoid(gc) * u).astype(o_ref.dtype)
            start += sub

    @pl.when(pl.program_id(1) == 0)
    def _():
        hh = _rms(halo_ref[...], g_ref[...])
        h_ref[:HALO, :] = jnp.where(inside, hh, 0.0).astype(BF16)
        row_blocks(True)

    @pl.when(pl.program_id(1) != 0)
    def _():
        row_blocks(False)


def ffn_in_grid(m, f, tm, tn):
    return (m // tm, f // tn)


def ffn_in(x, gain, w_in, layer, conv_w, conv_b, seq, tm, tn, subs, riders=()):
    m, d = x.shape
    assert sum(subs) == tm
    f = w_in.shape[2] // 2
    grid = ffn_in_grid(m, f, tm, tn)
    nj = grid[1]
    halo_per_block = tm // HALO
    r_in, r_out, r_shapes = _rider_specs(riders, grid)
    o, *cast = pl.pallas_call(
        functools.partial(_ffn_in_kernel, subs=subs, blocks_per_seq=seq // tm,
                          rider_inner_axes=tuple(tuple(range(r.axes, len(grid))) for r in riders)),
        grid=grid,
        in_specs=[
            pl.BlockSpec((tm, d), lambda i, j: (i, 0)),
            pl.BlockSpec((HALO, d), lambda i, j: (jnp.maximum(i * halo_per_block - 1, 0), 0)),
            pl.BlockSpec((1, d), lambda i, j: (0, 0)),
            pl.BlockSpec((None, d, tn), lambda i, j: (layer, 0, j)),
            pl.BlockSpec((None, d, tn), lambda i, j: (layer, 0, nj + j)),
            pl.BlockSpec((CONV_W, tn), lambda i, j: (0, j)),
            pl.BlockSpec((1, tn), lambda i, j: (0, j)),
        ] + r_in,
        out_specs=[pl.BlockSpec((tm, tn), lambda i, j: (i, j))] + r_out,
        out_shape=[jax.ShapeDtypeStruct((m, f), BF16)] + r_shapes,
        scratch_shapes=[pltpu.VMEM((HALO + tm, d), BF16), pltpu.VMEM((HALO + tm, tn), F32)],
        compiler_params=_params("parallel", "arbitrary"),
        name="ffn_in",
    )(x, x, gain, w_in, w_in, conv_w, conv_b, *[r.src for r in riders])
    return o, cast


def kernel(x, mem, mem_norm, norm_mix, norm_xattn, norm_ffn, ab_w_in, hgrn_lb_logits, hgrn_norm, gla_w_gk, gla_b_gk, gla_norm, ab_w_out, sb_w_qkv, sb_w_out, xa_w_q, xa_w_kv, xa_w_o, ffn_w_in, ffn_conv_w, ffn_conv_b, ffn_w_out, final_norm):
    b, s, d = x.shape
    depth = norm_mix.shape[0]
    m_len = mem.shape[1]
    rows = b * s
    tm = min(512, s)
    tm_big = min(1024, rows)
    tm_ffn = min(1024, s)
    ffn_subs = (tm_ffn // 2, tm_ffn // 2)
    ts = min(2048, s)
    tq_sb = min(256, s)

    def row(v):
        return v.reshape(1, -1)

    xr = x.reshape(rows, d)

    stacks = dict(ab_w_in=ab_w_in, ab_w_out=ab_w_out, sb_w_qkv=sb_w_qkv, sb_w_out=sb_w_out,
                  xa_w_q=xa_w_q, xa_w_kv=xa_w_kv, xa_w_o=xa_w_o, ffn_w_in=ffn_w_in,
                  ffn_w_out=ffn_w_out)
    ready = {}

    def weight(name, idx):
        if (name, idx) not in ready:
            ready[(name, idx)] = stacks[name][idx:idx + 1].astype(BF16)
        return ready[(name, idx)]

    def plan(wanted, grid, axes=None):
        keys = [key for key in wanted if key not in ready and key[1] < stacks[key[0]].shape[0]]
        riders = [_plan_rider(stacks[name], idx, grid, axes) for name, idx in keys]
        keys = [key for key, r in zip(keys, riders) if r is not None]
        return [r for r in riders if r is not None], keys

    def adopt(keys, cast):
        for key, w_bf16 in zip(keys, cast):
            ready[key] = w_bf16[None]

    ab_main = 4 * HGRN_HEADS * HGRN_HEAD_DIM + GLA_HEADS * (2 * GLA_HEAD_K + 2 * GLA_HEAD_V)
    mem_rows = mem.reshape(b * m_len, d)
    d_ff = ffn_w_in.shape[2] // 2

    for layer in range(depth):
        gain = row(norm_mix[layer])
        xattn_weights = [("xa_w_q", layer), ("xa_w_kv", layer), ("xa_w_o", layer)]
        if layer % 2 == 0:
            a = layer // 2
            w_low = jnp.pad(ab_w_in[a][:, ab_main:], ((0, 0), (0, LANES - GLA_GATE_RANK))).astype(BF16)
            w_gk = jnp.pad(gla_w_gk[a], ((0, LANES - GLA_GATE_RANK), (0, 0))).astype(BF16)
            p, gk = ab_proj(xr, gain, weight("ab_w_in", a), 0, ab_main, w_low, w_gk,
                            row(gla_b_gk[a]), tm=tm_big, tn=1024)
            p = p.reshape(b, s, ab_main)
            riders, keys = plan([("ab_w_out", a)] + xattn_weights, (b, HGRN_HEADS, s // ts))
            o_a, cast = hgrn_mixer(p, hgrn_lb_logits, a, row(hgrn_norm[a]), ts, riders)
            adopt(keys, cast)
            o_b = gla_mixer(p, gk.reshape(b, s, -1), row(gla_norm[a]), ts)
            xr = matmul2_resid(o_a.reshape(rows, -1), o_b.reshape(rows, -1),
                               weight("ab_w_out", a), 0, xr, tm=tm, tn=d)
        else:
            c = layer // 2
            qkv = norm_matmul(xr, gain, weight("sb_w_qkv", c), 0, BF16, tm=tm_big, tn=1024,
                              lead_cols=d, lead_scale=(SB_HEAD_DIM ** -0.5) * LOG2_E)
            riders, keys = plan([("sb_w_out", c)] + xattn_weights
                                + [("ffn_w_in", layer), ("ffn_w_out", layer)],
                                (b, SB_HEADS // SB_HEADS_PER_STEP, s // tq_sb))
            o, cast = sb_attention(qkv.reshape(b, s, -1), tq_sb, SB_HEADS_PER_STEP, riders)
            adopt(keys, cast)
            xr = matmul_resid(o.reshape(rows, d), weight("sb_w_out", c), 0, xr, tm=tm, tn=d)

        kv = norm_matmul(mem_rows, row(mem_norm), weight("xa_w_kv", layer), 0, BF16,
                         tm=min(1024, b * m_len), tn=d)
        riders, keys = plan([("ffn_w_in", layer)], (rows // tm,))
        xr, cast = xattn_layer(xr, row(norm_xattn[layer]), weight("xa_w_q", layer),
                               kv.reshape(b, m_len, 2 * d), weight("xa_w_o", layer), 0, seq=s,
                               tm=tm, riders=riders)
        adopt(keys, cast)

        ffn_grid = ffn_in_grid(rows, d_ff, tm_ffn, 512)
        riders, keys = plan([("ffn_w_out", layer)], ffn_grid)
        first_next = ("sb_w_qkv", layer // 2) if layer % 2 == 0 else ("ab_w_in", layer // 2 + 1)
        riders_next, keys_next = plan([first_next], ffn_grid, axes=1)
        act, cast = ffn_in(xr, row(norm_ffn[layer]), weight("ffn_w_in", layer), 0,
                           ffn_conv_w[layer], row(ffn_conv_b[layer]), seq=s, tm=tm_ffn, tn=512,
                           subs=ffn_subs, riders=riders + riders_next)
        adopt(keys + keys_next, cast)
        if layer < depth - 1:
            xr = matmul_resid(act, weight("ffn_w_out", layer), 0, xr, tm=tm_big, tn=512)
        else:
            xr = matmul_resid_norm(act, weight("ffn_w_out", layer), 0, xr, row(final_norm),
                                   tm=min(256, rows))

    return xr.reshape(b, s, d)
```

```python
import functools
from typing import NamedTuple

import jax
import jax.numpy as jnp
from jax import lax
from jax.experimental import pallas as pl
from jax.experimental.pallas import tpu as pltpu

F32 = jnp.float32
BF16 = jnp.bfloat16

RMS_EPS = 1e-6
CHUNK = 64
HGRN_HEADS = 8
HGRN_HEAD_DIM = 128
GLA_HEADS = 4
GLA_HEAD_K = 128
GLA_HEAD_V = 256
GLA_GATE_RANK = 16
GLA_GATE_NORMALIZER = 16.0
SB_HEADS = 16
SB_HEAD_DIM = 128
XA_HEADS = 4
CONV_W = 3
LOG2_E = 1.4426950408889634
EXP2_CLAMP = 126.0
SB_HEADS_PER_STEP = 8

LANES = 128
BF16_SUBLANES = 16
VMEM_LIMIT_BYTES = 52 * 1024 * 1024
NORM_SUB_ROWS = 256


def _params(*semantics):
    return pltpu.CompilerParams(dimension_semantics=semantics, vmem_limit_bytes=VMEM_LIMIT_BYTES)


def _rms(x, gain):
    return x * lax.rsqrt(jnp.mean(x * x, axis=-1, keepdims=True) + RMS_EPS) * gain


def _sigmoid(x):
    return 1.0 / (1.0 + jnp.exp(-x))


def _softplus(x):
    return jnp.maximum(x, 0.0) + jnp.log(1.0 + jnp.exp(-jnp.abs(x)))


def _dot(a, b):
    return jnp.dot(a, b, preferred_element_type=F32)


def _dot_nt(a, b):
    return lax.dot_general(a, b, (((1,), (1,)), ((), ())), preferred_element_type=F32)


def _dot_tn(a, b):
    return lax.dot_general(a, b, (((0,), (0,)), ((), ())), preferred_element_type=F32)


def _split3(x):
    hi = x.astype(BF16)
    r1 = x - hi.astype(F32)
    mid = r1.astype(BF16)
    lo = (r1 - mid.astype(F32)).astype(BF16)
    return hi, mid, lo


class _Rider(NamedTuple):
    src: jax.Array
    layer: int
    row_blocks: int
    col_blocks: int
    axes: int


def _plan_rider(src, layer, grid, axes=None):
    axes = len(grid) if axes is None else axes
    steps = 1
    for g in grid[:axes]:
        steps *= g
    k, n = src.shape[-2:]
    for col_blocks in range(1, steps + 1):
        if steps % col_blocks or n % (col_blocks * LANES):
            continue
        row_blocks = steps // col_blocks
        if k % (row_blocks * BF16_SUBLANES) == 0:
            return _Rider(src, layer, row_blocks, col_blocks, axes)
    return None


def _rider_specs(riders, grid):
    def step(ids):
        lin = ids[0]
        for g, i in zip(grid[1:len(ids)], ids[1:]):
            lin = lin * g + i
        return lin

    in_specs, out_specs, out_shapes = [], [], []
    for r in riders:
        k, n = r.src.shape[-2:]
        block = (k // r.row_blocks, n // r.col_blocks)

        def block_index(*ids, cb=r.col_blocks, axes=r.axes):
            lin = step(ids[:axes])
            return lin // cb, lin % cb

        def src_index(*ids, layer=r.layer, f=block_index):
            return (layer,) + tuple(f(*ids))

        in_specs.append(pl.BlockSpec((None,) + block, src_index))
        out_specs.append(pl.BlockSpec(block, block_index))
        out_shapes.append(jax.ShapeDtypeStruct((k, n), BF16))
    return in_specs, out_specs, out_shapes


def _cast_riders(src_refs, dst_refs, inner_axes=()):
    for n, (src_ref, dst_ref) in enumerate(zip(src_refs, dst_refs)):
        def cast(src_ref=src_ref, dst_ref=dst_ref):
            dst_ref[...] = src_ref[...].astype(BF16)

        held = inner_axes[n] if inner_axes else ()
        if held:
            first = pl.program_id(held[0]) == 0
            for axis in held[1:]:
                first = jnp.logical_and(first, pl.program_id(axis) == 0)
            pl.when(first)(cast)
        else:
            cast()


def _row_blocks(total, size):
    return [slice(r, r + size) for r in range(0, total, size)]


def _norm_matmul_kernel(x_ref, g_ref, w_ref, o_ref, h_ref, *, lead_blocks, lead_scale, sub):
    j = pl.program_id(1)

    def project(h):
        acc = _dot(h, w_ref[...])
        if lead_blocks:
            acc = acc * jnp.where(j < lead_blocks, lead_scale, 1.0)
        return acc.astype(o_ref.dtype)

    @pl.when(j == 0)
    def _():
        for rows in _row_blocks(x_ref.shape[0], sub):
            h = _rms(x_ref[rows, :], g_ref[...]).astype(BF16)
            h_ref[rows, :] = h
            o_ref[rows, :] = project(h)

    @pl.when(j != 0)
    def _():
        o_ref[...] = project(h_ref[...])


def norm_matmul(x, gain, w, layer, out_dtype, tm, tn, lead_cols=0, lead_scale=1.0):
    m, k = x.shape
    n = w.shape[2]
    assert lead_cols % tn == 0
    return pl.pallas_call(
        functools.partial(_norm_matmul_kernel, lead_blocks=lead_cols // tn, lead_scale=lead_scale,
                          sub=min(NORM_SUB_ROWS, tm)),
        grid=(m // tm, n // tn),
        in_specs=[
            pl.BlockSpec((tm, k), lambda i, j: (i, 0)),
            pl.BlockSpec((1, k), lambda i, j: (0, 0)),
            pl.BlockSpec((None, k, tn), lambda i, j: (layer, 0, j)),
        ],
        out_specs=pl.BlockSpec((tm, tn), lambda i, j: (i, j)),
        out_shape=jax.ShapeDtypeStruct((m, n), out_dtype),
        scratch_shapes=[pltpu.VMEM((tm, k), BF16)],
        compiler_params=_params("parallel", "arbitrary"),
        name="norm_matmul",
    )(x, gain, w)


def _ab_proj_kernel(x_ref, g_ref, w_ref, wlow_ref, wgk_ref, bgk_ref, o_ref, gk_ref, h_ref, *, sub):
    @pl.when(pl.program_id(1) == 0)
    def _():
        for rows in _row_blocks(x_ref.shape[0], sub):
            h = _rms(x_ref[rows, :], g_ref[...]).astype(BF16)
            h_ref[rows, :] = h
            o_ref[rows, :] = _dot(h, w_ref[...])
            low = _dot(h, wlow_ref[...])
            pre = _dot(low.astype(BF16), wgk_ref[...]) + bgk_ref[...]
            gk_ref[rows, :] = -_softplus(-pre) * (LOG2_E / GLA_GATE_NORMALIZER)

    @pl.when(pl.program_id(1) != 0)
    def _():
        o_ref[...] = _dot(h_ref[...], w_ref[...])


def ab_proj(x, gain, w, layer, n, w_low, w_gk, b_gk, tm, tn):
    m, k = x.shape
    assert n % tn == 0
    ngk = w_gk.shape[1]
    return pl.pallas_call(
        functools.partial(_ab_proj_kernel, sub=min(NORM_SUB_ROWS, tm)),
        grid=(m // tm, n // tn),
        in_specs=[
            pl.BlockSpec((tm, k), lambda i, j: (i, 0)),
            pl.BlockSpec((1, k), lambda i, j: (0, 0)),
            pl.BlockSpec((None, k, tn), lambda i, j: (layer, 0, j)),
            pl.BlockSpec((k, LANES), lambda i, j: (0, 0)),
            pl.BlockSpec((LANES, ngk), lambda i, j: (0, 0)),
            pl.BlockSpec((1, ngk), lambda i, j: (0, 0)),
        ],
        out_specs=[
            pl.BlockSpec((tm, tn), lambda i, j: (i, j)),
            pl.BlockSpec((tm, ngk), lambda i, j: (i, 0)),
        ],
        out_shape=[
            jax.ShapeDtypeStruct((m, n), F32),
            jax.ShapeDtypeStruct((m, ngk), F32),
        ],
        scratch_shapes=[pltpu.VMEM((tm, k), BF16)],
        compiler_params=_params("parallel", "arbitrary"),
        name="ab_proj",
    )(x, gain, w, w_low, w_gk, b_gk)


def _matmul_resid_kernel(a_ref, w_ref, r_ref, o_ref):
    o_ref[...] = r_ref[...] + _dot(a_ref[...], w_ref[...])


def matmul_resid(a, w, layer, resid, tm, tn):
    m, k = a.shape
    n = w.shape[2]
    return pl.pallas_call(
        _matmul_resid_kernel,
        grid=(m // tm, n // tn),
        in_specs=[
            pl.BlockSpec((tm, k), lambda i, j: (i, 0)),
            pl.BlockSpec((None, k, tn), lambda i, j: (layer, 0, j)),
            pl.BlockSpec((tm, tn), lambda i, j: (i, j)),
        ],
        out_specs=pl.BlockSpec((tm, tn), lambda i, j: (i, j)),
        out_shape=jax.ShapeDtypeStruct((m, n), F32),
        compiler_params=_params("parallel", "arbitrary"),
        name="matmul_resid",
    )(a, w, resid)


def _matmul_resid_norm_kernel(a_ref, w_ref, r_ref, g_ref, o_ref, *, sub):
    for rows in _row_blocks(a_ref.shape[0], sub):
        y = r_ref[rows, :] + _dot(a_ref[rows, :], w_ref[...])
        o_ref[rows, :] = _rms(y, g_ref[...])


def matmul_resid_norm(a, w, layer, resid, gain, tm):
    m, k = a.shape
    n = w.shape[2]
    return pl.pallas_call(
        functools.partial(_matmul_resid_norm_kernel, sub=tm // 2),
        grid=(m // tm,),
        in_specs=[
            pl.BlockSpec((tm, k), lambda i: (i, 0)),
            pl.BlockSpec((None, k, n), lambda i: (layer, 0, 0), pipeline_mode=pl.Buffered(1)),
            pl.BlockSpec((tm, n), lambda i: (i, 0)),
            pl.BlockSpec((1, n), lambda i: (0, 0)),
        ],
        out_specs=pl.BlockSpec((tm, n), lambda i: (i, 0)),
        out_shape=jax.ShapeDtypeStruct((m, n), F32),
        compiler_params=_params("parallel"),
        name="matmul_resid_norm",
    )(a, w, resid, gain)


def _matmul2_resid_kernel(a1_ref, a2_ref, w1_ref, w2_ref, r_ref, o_ref):
    o_ref[...] = r_ref[...] + (_dot(a1_ref[...], w1_ref[...]) + _dot(a2_ref[...], w2_ref[...]))


def matmul2_resid(a1, a2, w, layer, resid, tm, tn):
    m, kh = a1.shape
    n = w.shape[2]
    return pl.pallas_call(
        _matmul2_resid_kernel,
        grid=(m // tm, n // tn),
        in_specs=[
            pl.BlockSpec((tm, kh), lambda i, j: (i, 0)),
            pl.BlockSpec((tm, kh), lambda i, j: (i, 0)),
            pl.BlockSpec((None, kh, tn), lambda i, j: (layer, 0, j)),
            pl.BlockSpec((None, kh, tn), lambda i, j: (layer, 1, j)),
            pl.BlockSpec((tm, tn), lambda i, j: (i, j)),
        ],
        out_specs=pl.BlockSpec((tm, tn), lambda i, j: (i, j)),
        out_shape=jax.ShapeDtypeStruct((m, n), F32),
        compiler_params=_params("parallel", "arbitrary"),
        name="matmul2_resid",
    )(a1, a2, w, w, resid)


def _linear_attention_block(q, k, v, g2, state_ref):
    n = q.shape[0] // CHUNK
    row = lax.broadcasted_iota(jnp.int32, (CHUNK, CHUNK), 0)
    col = lax.broadcasted_iota(jnp.int32, (CHUNK, CHUNK), 1)
    causal = row >= col
    tri = causal.astype(BF16)
    sl = [slice(c * CHUNK, (c + 1) * CHUNK) for c in range(n)]

    g_terms = [_split3(g2[s]) for s in sl]
    b = [_dot(tri, hi) + _dot(tri, mid) + _dot(tri, lo) for hi, mid, lo in g_terms]
    b_mid = [x[CHUNK // 2 - 1:CHUNK // 2] for x in b]
    b_last = [x[CHUNK - 1:] for x in b]
    vb = [v[s].astype(BF16) for s in sl]
    qs = [(q[s] * jnp.exp2(x - m)).astype(BF16) for s, x, m in zip(sl, b, b_mid)]
    ks = [(k[s] * jnp.exp2(m - x)).astype(BF16) for s, x, m in zip(sl, b, b_mid)]
    kd = [(k[s] * jnp.exp2(l - x)).astype(BF16) for s, x, l in zip(sl, b, b_last)]
    qd = [(q[s] * jnp.exp2(x)).astype(BF16) for s, x in zip(sl, b)]

    scores = [jnp.where(causal, _dot_nt(a, c), 0.0).astype(BF16) for a, c in zip(qs, ks)]
    update = [_dot_tn(a, c) for a, c in zip(vb, kd)]
    o_intra = [_dot(a, c) for a, c in zip(scores, vb)]

    state = state_ref[...]
    states = []
    for c in range(n):
        states.append(state.astype(BF16))
        state = state * jnp.exp2(b_last[c]) + update[c]
    state_ref[...] = state

    o = [oi + _dot_nt(a, st) for oi, a, st in zip(o_intra, qd, states)]
    return jnp.concatenate(o, axis=0)


def _hgrn_kernel(*refs, lb_rows, n_riders):
    aq_ref, af_ref, ai_ref, ag_ref, lbl_ref, gain_ref = refs[:6]
    o_ref, state_ref = refs[6 + n_riders], refs[-1]
    _cast_riders(refs[6:6 + n_riders], refs[7 + n_riders:-1])

    @pl.when(pl.program_id(2) == 0)
    def _():
        state_ref[...] = jnp.zeros_like(state_ref)

    logits = lbl_ref[...]
    e = jnp.exp(logits - jnp.max(logits, axis=0, keepdims=True))
    lb = jnp.sum(e[:lb_rows], axis=0, keepdims=True) / jnp.sum(e, axis=0, keepdims=True)

    a_q = aq_ref[0]
    f = lb + (1.0 - lb) * _sigmoid(af_ref[0])
    o = _linear_attention_block(a_q * _sigmoid(a_q), 1.0 - f, ai_ref[0], jnp.log(f) * LOG2_E,
                                state_ref)
    o_ref[0] = (_rms(o, gain_ref[...]) * _sigmoid(ag_ref[0])).astype(o_ref.dtype)


def hgrn_mixer(p, lb_logits, a_idx, gain, ts, riders=()):
    b, s, _ = p.shape
    nh, hd = HGRN_HEADS, HGRN_HEAD_DIM
    nlb = lb_logits.shape[0]
    grid = (b, nh, s // ts)
    r_in, r_out, r_shapes = _rider_specs(riders, grid)

    def col(part):
        return pl.BlockSpec((1, ts, hd), lambda bi, h, si: (bi, si, part * nh + h))

    o, *cast = pl.pallas_call(
        functools.partial(_hgrn_kernel, lb_rows=a_idx + 1, n_riders=len(riders)),
        grid=grid,
        in_specs=[
            col(0), col(1), col(2), col(3),
            pl.BlockSpec((nlb, hd), lambda bi, h, si: (0, h)),
            pl.BlockSpec((1, hd), lambda bi, h, si: (0, 0)),
        ] + r_in,
        out_specs=[pl.BlockSpec((1, ts, hd), lambda bi, h, si: (bi, si, h))] + r_out,
        out_shape=[jax.ShapeDtypeStruct((b, s, nh * hd), BF16)] + r_shapes,
        scratch_shapes=[pltpu.VMEM((hd, hd), F32)],
        compiler_params=_params("parallel", "parallel", "arbitrary"),
        name="hgrn_mixer",
    )(p, p, p, p, lb_logits, gain, *[r.src for r in riders])
    return o, cast


def _gla_kernel(q_ref, k_ref, v_ref, gg_ref, gk_ref, gain_ref, o_ref, state_ref):
    @pl.when(pl.program_id(2) == 0)
    def _():
        state_ref[...] = jnp.zeros_like(state_ref)

    q = q_ref[0] * (GLA_HEAD_K ** -0.5)
    o = _linear_attention_block(q, k_ref[0], v_ref[0], gk_ref[0], state_ref)
    g_g = gg_ref[0]
    o_ref[0] = (_rms(o, gain_ref[...]) * (g_g * _sigmoid(g_g))).astype(o_ref.dtype)


def gla_mixer(p, gk, gain, ts):
    b, s, _ = p.shape
    nh, kd, vd = GLA_HEADS, GLA_HEAD_K, GLA_HEAD_V
    base = 4 * HGRN_HEADS * HGRN_HEAD_DIM
    q0 = base // kd
    k0 = q0 + nh
    v0 = (base + 2 * nh * kd) // vd
    g0 = v0 + nh
    return pl.pallas_call(
        _gla_kernel,
        grid=(b, nh, s // ts),
        in_specs=[
            pl.BlockSpec((1, ts, kd), lambda bi, h, si: (bi, si, q0 + h)),
            pl.BlockSpec((1, ts, kd), lambda bi, h, si: (bi, si, k0 + h)),
            pl.BlockSpec((1, ts, vd), lambda bi, h, si: (bi, si, v0 + h)),
            pl.BlockSpec((1, ts, vd), lambda bi, h, si: (bi, si, g0 + h)),
            pl.BlockSpec((1, ts, kd), lambda bi, h, si: (bi, si, h)),
            pl.BlockSpec((1, vd), lambda bi, h, si: (0, 0)),
        ],
        out_specs=pl.BlockSpec((1, ts, vd), lambda bi, h, si: (bi, si, h)),
        out_shape=jax.ShapeDtypeStruct((b, s, nh * vd), BF16),
        scratch_shapes=[pltpu.VMEM((vd, kd), F32)],
        compiler_params=_params("parallel", "parallel", "arbitrary"),
        name="gla_mixer",
    )(p, p, p, p, gk, gain)


def _sb_kernel(*refs, tq, heads, n_riders):
    q_ref, k_ref, v_ref = refs[:3]
    o_ref = refs[3 + n_riders]
    _cast_riders(refs[3:3 + n_riders], refs[4 + n_riders:])

    qi = pl.program_id(2)
    hd = SB_HEAD_DIM
    row = lax.broadcasted_iota(jnp.int32, (tq, tq), 0)
    col = lax.broadcasted_iota(jnp.int32, (tq, tq), 1)
    behind_ones = (row > col).astype(BF16)
    strictly_causal = col < row
    head_cols = [slice(h * hd, (h + 1) * hd) for h in range(heads)]

    def key_block(j, carry, diagonal):
        start = pl.multiple_of(j * tq, tq)
        acc = [c[0] for c in carry]
        later = [c[1] for c in carry]
        z2 = [_dot_nt(q_ref[0, :, hc], k_ref[0, pl.ds(start, tq), hc]) for hc in head_cols]
        sp2 = [jnp.maximum(z, jnp.log(1.0 + jnp.exp2(jnp.minimum(z, EXP2_CLAMP))) * LOG2_E)
               for z in z2]
        if diagonal:
            sp2 = [jnp.where(strictly_causal, x, 0.0) for x in sp2]
        own = [z - x - lt for z, x, lt in zip(z2, sp2, later)]
        behind = [_dot(x.astype(BF16), behind_ones) for x in sp2]
        log2_a = [o - bh for o, bh in zip(own, behind)]
        if diagonal:
            log2_a = [jnp.where(strictly_causal, x, -jnp.inf) for x in log2_a]
        att = [jnp.exp2(x).astype(BF16) for x in log2_a]
        acc = [a + _dot(p, v_ref[0, pl.ds(start, tq), hc]) for a, p, hc in zip(acc, att, head_cols)]
        later = [lt + (x[:, 0:1] + bh[:, 0:1]) for lt, x, bh in zip(later, sp2, behind)]
        return tuple(zip(acc, later))

    zero = (jnp.zeros((tq, hd), F32), jnp.zeros((tq, 1), F32))
    carry = key_block(qi, (zero,) * heads, True)
    carry = lax.fori_loop(0, qi, lambda t, c: key_block(qi - 1 - t, c, False), carry)
    for h, hc in enumerate(head_cols):
        o_ref[0, :, hc] = carry[h][0].astype(o_ref.dtype)


def sb_attention(qkv, tq, heads, riders=()):
    b, s, _ = qkv.shape
    ng = SB_HEADS // heads
    w = heads * SB_HEAD_DIM
    grid = (b, ng, s // tq)
    r_in, r_out, r_shapes = _rider_specs(riders, grid)
    o, *cast = pl.pallas_call(
        functools.partial(_sb_kernel, tq=tq, heads=heads, n_riders=len(riders)),
        grid=grid,
        in_specs=[
            pl.BlockSpec((1, tq, w), lambda bi, g, qi: (bi, qi, g)),
            pl.BlockSpec((1, s, w), lambda bi, g, qi: (bi, 0, ng + g)),
            pl.BlockSpec((1, s, w), lambda bi, g, qi: (bi, 0, 2 * ng + g)),
        ] + r_in,
        out_specs=[pl.BlockSpec((1, tq, w), lambda bi, g, qi: (bi, qi, g))] + r_out,
        out_shape=[jax.ShapeDtypeStruct((b, s, SB_HEADS * SB_HEAD_DIM), BF16)] + r_shapes,
        compiler_params=_params("parallel", "parallel", "arbitrary"),
        name="sb_attention",
    )(qkv, qkv, qkv, *[r.src for r in riders])
    return o, cast


def _xattn_layer_kernel(*refs, n_riders):
    x_ref, g_ref, wq_ref, k_ref, v_ref, wo_ref = refs[:6]
    o_ref = refs[6 + n_riders]
    _cast_riders(refs[6:6 + n_riders], refs[7 + n_riders:])

    x = x_ref[...]
    d = x.shape[-1]
    hd = d // XA_HEADS
    head_cols = [slice(h * hd, (h + 1) * hd) for h in range(XA_HEADS)]
    h_in = [_rms(x[rows], g_ref[...]).astype(BF16)
            for rows in _row_blocks(x.shape[0], min(NORM_SUB_ROWS, x.shape[0]))]
    q = [jnp.concatenate([_dot(hp, wq_ref[:, hc]).astype(BF16) for hp in h_in], axis=0)
         for hc in head_cols]
    scores = [_dot_nt(qh, k_ref[0, :, hc]) * (hd ** -0.5) for qh, hc in zip(q, head_cols)]
    e = [jnp.exp(sc - jnp.max(sc, axis=-1, keepdims=True)) for sc in scores]
    probs = [(eh / jnp.sum(eh, axis=-1, keepdims=True)).astype(BF16) for eh in e]
    o = [_dot(ph, v_ref[0, :, hc]).astype(BF16) for ph, hc in zip(probs, head_cols)]
    o_ref[...] = x + _dot(jnp.concatenate(o, axis=1), wo_ref[...])


def xattn_layer(x, gain, w_q, kv, w_o, layer, seq, tm, riders=()):
    rows, d = x.shape
    m = kv.shape[1]
    blocks_per_seq = seq // tm
    grid = (rows // tm,)
    r_in, r_out, r_shapes = _rider_specs(riders, grid)
    whole = functools.partial(pl.BlockSpec, (None, d, d), lambda i: (layer, 0, 0),
                              pipeline_mode=pl.Buffered(1))
    o, *cast = pl.pallas_call(
        functools.partial(_xattn_layer_kernel, n_riders=len(riders)),
        grid=grid,
        in_specs=[
            pl.BlockSpec((tm, d), lambda i: (i, 0)),
            pl.BlockSpec((1, d), lambda i: (0, 0)),
            whole(),
            pl.BlockSpec((1, m, d), lambda i: (i // blocks_per_seq, 0, 0)),
            pl.BlockSpec((1, m, d), lambda i: (i // blocks_per_seq, 0, 1)),
            whole(),
        ] + r_in,
        out_specs=[pl.BlockSpec((tm, d), lambda i: (i, 0))] + r_out,
        out_shape=[jax.ShapeDtypeStruct((rows, d), F32)] + r_shapes,
        compiler_params=_params("parallel"),
        name="xattn_layer",
    )(x, gain, w_q, kv, kv, w_o, *[r.src for r in riders])
    return o, cast


HALO = BF16_SUBLANES


def _ffn_in_kernel(*refs, subs, blocks_per_seq, rider_inner_axes):
    n_riders = len(rider_inner_axes)
    x_ref, halo_ref, g_ref, wu_ref, wg_ref, cw_ref, cb_ref = refs[:7]
    o_ref, h_ref, gs_ref = refs[7 + n_riders], refs[-2], refs[-1]
    _cast_riders(refs[7:7 + n_riders], refs[8 + n_riders:-2], rider_inner_axes)

    inside = pl.program_id(0) % blocks_per_seq != 0
    cw = cw_ref[...]
    cb = cb_ref[...]

    def row_blocks(normalise):
        start = 0
        for sub in subs:
            lo = HALO + start
            first = 0 if start == 0 else lo
            if normalise:
                h_ref[lo:lo + sub, :] = _rms(x_ref[start:start + sub, :], g_ref[...]).astype(BF16)
            gs_ref[first:lo + sub, :] = _dot(h_ref[first:lo + sub, :], wg_ref[...])
            u = _dot(h_ref[lo:lo + sub, :], wu_ref[...])
            conv = 0.0
            for tap in range(CONV_W):
                t0 = lo - (CONV_W - 1) + tap
                conv = conv + cw[tap:tap + 1] * gs_ref[t0:t0 + sub, :]
            gc = cb + conv
            o_ref[start:start + sub, :] = (gc * _sigmoid(gc) * u).astype(o_ref.dtype)
            start += sub

    @pl.when(pl.program_id(1) == 0)
    def _():
        hh = _rms(halo_ref[...], g_ref[...])
        h_ref[:HALO, :] = jnp.where(inside, hh, 0.0).astype(BF16)
        row_blocks(True)

    @pl.when(pl.program_id(1) != 0)
    def _():
        row_blocks(False)


def ffn_in_grid(m, f, tm, tn):
    return (m // tm, f // tn)


def ffn_in(x, gain, w_in, layer, conv_w, conv_b, seq, tm, tn, subs, riders=()):
    m, d = x.shape
    assert sum(subs) == tm
    f = w_in.shape[2] // 2
    grid = ffn_in_grid(m, f, tm, tn)
    nj = grid[1]
    halo_per_block = tm // HALO
    r_in, r_out, r_shapes = _rider_specs(riders, grid)
    o, *cast = pl.pallas_call(
        functools.partial(_ffn_in_kernel, subs=subs, blocks_per_seq=seq // tm,
                          rider_inner_axes=tuple(tuple(range(r.axes, len(grid))) for r in riders)),
        grid=grid,
        in_specs=[
            pl.BlockSpec((tm, d), lambda i, j: (i, 0)),
            pl.BlockSpec((HALO, d), lambda i, j: (jnp.maximum(i * halo_per_block - 1, 0), 0)),
            pl.BlockSpec((1, d), lambda i, j: (0, 0)),
            pl.BlockSpec((None, d, tn), lambda i, j: (layer, 0, j)),
            pl.BlockSpec((None, d, tn), lambda i, j: (layer, 0, nj + j)),
            pl.BlockSpec((CONV_W, tn), lambda i, j: (0, j)),
            pl.BlockSpec((1, tn), lambda i, j: (0, j)),
        ] + r_in,
        out_specs=[pl.BlockSpec((tm, tn), lambda i, j: (i, j))] + r_out,
        out_shape=[jax.ShapeDtypeStruct((m, f), BF16)] + r_shapes,
        scratch_shapes=[pltpu.VMEM((HALO + tm, d), BF16), pltpu.VMEM((HALO + tm, tn), F32)],
        compiler_params=_params("parallel", "arbitrary"),
        name="ffn_in",
    )(x, x, gain, w_in, w_in, conv_w, conv_b, *[r.src for r in riders])
    return o, cast


def kernel(x, mem, mem_norm, norm_mix, norm_xattn, norm_ffn, ab_w_in, hgrn_lb_logits, hgrn_norm, gla_w_gk, gla_b_gk, gla_norm, ab_w_out, sb_w_qkv, sb_w_out, xa_w_q, xa_w_kv, xa_w_o, ffn_w_in, ffn_conv_w, ffn_conv_b, ffn_w_out, final_norm):
    b, s, d = x.shape
    depth = norm_mix.shape[0]
    m_len = mem.shape[1]
    rows = b * s
    tm = min(512, s)
    tm_big = min(1024, rows)
    tm_ffn = min(1024, s)
    ffn_subs = (tm_ffn // 2, tm_ffn // 2)
    ts = min(2048, s)
    tq_sb = min(256, s)

    def row(v):
        return v.reshape(1, -1)

    xr = x.reshape(rows, d)

    stacks = dict(ab_w_in=ab_w_in, ab_w_out=ab_w_out, sb_w_qkv=sb_w_qkv, sb_w_out=sb_w_out,
                  xa_w_q=xa_w_q, xa_w_kv=xa_w_kv, xa_w_o=xa_w_o, ffn_w_in=ffn_w_in,
                  ffn_w_out=ffn_w_out)
    ready = {}

    def weight(name, idx):
        if (name, idx) not in ready:
            ready[(name, idx)] = stacks[name][idx:idx + 1].astype(BF16)
        return ready[(name, idx)]

    def plan(wanted, grid, axes=None):
        keys = [key for key in wanted if key not in ready and key[1] < stacks[key[0]].shape[0]]
        riders = [_plan_rider(stacks[name], idx, grid, axes) for name, idx in keys]
        keys = [key for key, r in zip(keys, riders) if r is not None]
        return [r for r in riders if r is not None], keys

    def adopt(keys, cast):
        for key, w_bf16 in zip(keys, cast):
            ready[key] = w_bf16[None]

    ab_main = 4 * HGRN_HEADS * HGRN_HEAD_DIM + GLA_HEADS * (2 * GLA_HEAD_K + 2 * GLA_HEAD_V)
    mem_rows = mem.reshape(b * m_len, d)
    d_ff = ffn_w_in.shape[2] // 2

    for layer in range(depth):
        gain = row(norm_mix[layer])
        xattn_weights = [("xa_w_q", layer), ("xa_w_kv", layer), ("xa_w_o", layer)]
        if layer % 2 == 0:
            a = layer // 2
            w_low = jnp.pad(ab_w_in[a][:, ab_main:], ((0, 0), (0, LANES - GLA_GATE_RANK))).astype(BF16)
            w_gk = jnp.pad(gla_w_gk[a], ((0, LANES - GLA_GATE_RANK), (0, 0))).astype(BF16)
            p, gk = ab_proj(xr, gain, weight("ab_w_in", a), 0, ab_main, w_low, w_gk,
                            row(gla_b_gk[a]), tm=tm_big, tn=1024)
            p = p.reshape(b, s, ab_main)
            riders, keys = plan([("ab_w_out", a)] + xattn_weights, (b, HGRN_HEADS, s // ts))
            o_a, cast = hgrn_mixer(p, hgrn_lb_logits, a, row(hgrn_norm[a]), ts, riders)
            adopt(keys, cast)
            o_b = gla_mixer(p, gk.reshape(b, s, -1), row(gla_norm[a]), ts)
            xr = matmul2_resid(o_a.reshape(rows, -1), o_b.reshape(rows, -1),
                               weight("ab_w_out", a), 0, xr, tm=tm, tn=d)
        else:
            c = layer // 2
            qkv = norm_matmul(xr, gain, weight("sb_w_qkv", c), 0, BF16, tm=tm_big, tn=1024,
                              lead_cols=d, lead_scale=(SB_HEAD_DIM ** -0.5) * LOG2_E)
            riders, keys = plan([("sb_w_out", c)] + xattn_weights
                                + [("ffn_w_in", layer), ("ffn_w_out", layer)],
                                (b, SB_HEADS // SB_HEADS_PER_STEP, s // tq_sb))
            o, cast = sb_attention(qkv.reshape(b, s, -1), tq_sb, SB_HEADS_PER_STEP, riders)
            adopt(keys, cast)
            xr = matmul_resid(o.reshape(rows, d), weight("sb_w_out", c), 0, xr, tm=tm, tn=d)

        kv = norm_matmul(mem_rows, row(mem_norm), weight("xa_w_kv", layer), 0, BF16,
                         tm=min(1024, b * m_len), tn=d)
        riders, keys = plan([("ffn_w_in", layer)], (rows // tm,))
        xr, cast = xattn_layer(xr, row(norm_xattn[layer]), weight("xa_w_q", layer),
                               kv.reshape(b, m_len, 2 * d), weight("xa_w_o", layer), 0, seq=s,
                               tm=tm, riders=riders)
        adopt(keys, cast)

        ffn_grid = ffn_in_grid(rows, d_ff, tm_ffn, 512)
        riders, keys = plan([("ffn_w_out", layer)], ffn_grid)
        first_next = ("sb_w_qkv", layer // 2) if layer % 2 == 0 else ("ab_w_in", layer // 2 + 1)
        riders_next, keys_next = plan([first_next], ffn_grid, axes=1)
        act, cast = ffn_in(xr, row(norm_ffn[layer]), weight("ffn_w_in", layer), 0,
                           ffn_conv_w[layer], row(ffn_conv_b[layer]), seq=s, tm=tm_ffn, tn=512,
                           subs=ffn_subs, riders=riders + riders_next)
        adopt(keys + keys_next, cast)
        if layer < depth - 1:
            xr = matmul_resid(act, weight("ffn_w_out", layer), 0, xr, tm=tm_big, tn=512)
        else:
            xr = matmul_resid_norm(act, weight("ffn_w_out", layer), 0, xr, row(final_norm),
                                   tm=min(256, rows))

    return xr.reshape(b, s, d)
```
